```python
import jax, jax.numpy as jnp
from jax import lax
import numpy as np

D_MODEL = 2048
BATCH = 1
SEQ = 16384
DEPTH = 1

MEM_LEN = 256
D_FF = 5632
CONV_CH = 1536
CONV_WIDTH = 31
MLA_HEADS = 12
Q_LORA = 512
KV_LORA = 512
QK_NOPE = 128
QK_ROPE = 64
QK_DIM = QK_NOPE + QK_ROPE
V_HEAD = 128
MEM_HEADS = 4
MEM_HEAD_DIM = 256
N_BRANCH = 3
ROPE_THETA = 10000.0
EPS = 1e-6
Q_BLOCK = 128
IN_SIZES = (2 * CONV_CH, Q_LORA, KV_LORA, QK_ROPE, MEM_HEADS * MEM_HEAD_DIM, N_BRANCH * D_MODEL)
IN_COLS = 2 * CONV_CH + Q_LORA + KV_LORA + QK_ROPE + MEM_HEADS * MEM_HEAD_DIM + N_BRANCH * D_MODEL

kernel_name = "hybrid_gated_conformer_mla_memory_layer"


def _split_points(sizes):
    pts, acc = [], 0
    for s in sizes[:-1]:
        acc += s
        pts.append(acc)
    return pts


def rmsnorm(x, g):
    xf = x.astype(jnp.float32)
    y = xf * lax.rsqrt(jnp.mean(xf * xf, axis=-1, keepdims=True) + EPS)
    return (y * g.astype(jnp.float32)).astype(x.dtype)


def layernorm(x, g, b):
    xf = x.astype(jnp.float32)
    mu = jnp.mean(xf, axis=-1, keepdims=True)
    xc = xf - mu
    y = xc * lax.rsqrt(jnp.mean(xc * xc, axis=-1, keepdims=True) + EPS)
    return (y * g.astype(jnp.float32) + b.astype(jnp.float32)).astype(x.dtype)


def swiglu_ffn(h, w_gu, w_down):
    g, u = jnp.split(h @ w_gu, 2, axis=-1)
    return (jax.nn.silu(g) * u) @ w_down


def rope_tables(positions):
    inv_freq = ROPE_THETA ** (-jnp.arange(0, QK_ROPE, 2, dtype=jnp.float32) / QK_ROPE)
    ang = positions.astype(jnp.float32)[..., None] * inv_freq
    return jnp.cos(ang)[:, :, None, :], jnp.sin(ang)[:, :, None, :]


def apply_rope(x, cos, sin):
    xf = x.astype(jnp.float32)
    x1, x2 = jnp.split(xf, 2, axis=-1)
    return jnp.concatenate([x1 * cos - x2 * sin, x2 * cos + x1 * sin], axis=-1).astype(x.dtype)


def causal_attention_blocked(q, k, v, scale):
    B, S, H, Dk = q.shape
    Dv = v.shape[-1]
    nb = S // Q_BLOCK
    qb = q.reshape(B, nb, Q_BLOCK, H, Dk).transpose(1, 0, 2, 3, 4)
    key_pos = jnp.arange(S)

    def one_block(args):
        i, q_i = args
        s = jnp.einsum('bqhd,bkhd->bhqk', q_i, k, preferred_element_type=jnp.float32) * scale
        q_pos = i * Q_BLOCK + jnp.arange(Q_BLOCK)
        mask = key_pos[None, :] <= q_pos[:, None]
        s = jnp.where(mask[None, None], s, -jnp.inf)
        p = jax.nn.softmax(s, axis=-1)
        return jnp.einsum('bhqk,bkhd->bqhd', p.astype(v.dtype), v)

    out = lax.map(one_block, (jnp.arange(nb), qb))
    return out.transpose(1, 0, 2, 3, 4).reshape(B, S, H, Dv)


def setup_inputs(seed: int = 0) -> dict:
    key = jax.random.key(seed)
    ks = iter(jax.random.split(key, 40))
    L = DEPTH

    def w(shape, fan_in):
        return jax.random.normal(next(ks), shape, jnp.float32) * (fan_in ** -0.5)

    def gain(shape):
        return 1.0 + 0.02 * jax.random.normal(next(ks), shape, jnp.float32)

    def bias(shape):
        return 0.01 * jax.random.normal(next(ks), shape, jnp.float32)

    x = jax.random.normal(next(ks), (BATCH, SEQ, D_MODEL), jnp.float32)
    mem = jax.random.normal(next(ks), (BATCH, MEM_LEN, D_MODEL), jnp.float32)
    positions = jnp.broadcast_to(jnp.arange(SEQ, dtype=jnp.int32)[None, :], (BATCH, SEQ))
    return {
        "x": x,
        "mem": mem,
        "positions": positions,
        "g_ffn1": gain((L, D_MODEL)),
        "w_ffn1_gu": w((L, D_MODEL, 2 * D_FF), D_MODEL),
        "w_ffn1_down": w((L, D_FF, D_MODEL), D_FF),
        "g_mix": gain((L, D_MODEL)),
        "w_in": w((L, D_MODEL, IN_COLS), D_MODEL),
        "b_gate": bias((L, N_BRANCH * D_MODEL)),
        "conv_w": w((L, CONV_WIDTH, CONV_CH), CONV_WIDTH),
        "conv_b": bias((L, CONV_CH)),
        "conv_ln_g": gain((L, CONV_CH)),
        "conv_ln_b": bias((L, CONV_CH)),
        "w_conv_out": w((L, CONV_CH, D_MODEL), CONV_CH),
        "g_q_a": gain((L, Q_LORA)),
        "w_uq": w((L, Q_LORA, MLA_HEADS * QK_DIM), Q_LORA),
        "g_kv_a": gain((L, KV_LORA)),
        "w_ukv": w((L, KV_LORA, MLA_HEADS * (QK_NOPE + V_HEAD)), KV_LORA),
        "g_qnorm": gain((L, QK_DIM)),
        "g_knorm": gain((L, QK_DIM)),
        "w_mla_out": w((L, MLA_HEADS * V_HEAD, D_MODEL), MLA_HEADS * V_HEAD),
        "g_mem": gain((L, D_MODEL)),
        "w_mem_kv": w((L, D_MODEL, 2 * MEM_HEADS * MEM_HEAD_DIM), D_MODEL),
        "g_mqnorm": gain((L, MEM_HEAD_DIM)),
        "g_mknorm": gain((L, MEM_HEAD_DIM)),
        "w_mem_out": w((L, MEM_HEADS * MEM_HEAD_DIM, D_MODEL), MEM_HEADS * MEM_HEAD_DIM),
        "w_out": w((L, D_MODEL, D_MODEL), D_MODEL),
        "g_ffn2": gain((L, D_MODEL)),
        "w_ffn2_gu": w((L, D_MODEL, 2 * D_FF), D_MODEL),
        "w_ffn2_down": w((L, D_FF, D_MODEL), D_FF),
    }


def reference(x, mem, positions, g_ffn1, w_ffn1_gu, w_ffn1_down, g_mix, w_in, b_gate,
              conv_w, conv_b, conv_ln_g, conv_ln_b, w_conv_out, g_q_a, w_uq, g_kv_a, w_ukv,
              g_qnorm, g_knorm, w_mla_out, g_mem, w_mem_kv, g_mqnorm, g_mknorm, w_mem_out,
              w_out, g_ffn2, w_ffn2_gu, w_ffn2_down):
    B, S, D = x.shape
    cos, sin = rope_tables(positions)
    split_pts = _split_points(IN_SIZES)
    mla_scale = QK_DIM ** -0.5
    mem_scale = MEM_HEAD_DIM ** -0.5

    for l in range(DEPTH):
        x = x + 0.5 * swiglu_ffn(rmsnorm(x, g_ffn1[l]), w_ffn1_gu[l], w_ffn1_down[l])

        h = rmsnorm(x, g_mix[l])
        conv_in, cq, ckv, k_pe, mq, gate_logits = jnp.split(h @ w_in[l], split_pts, axis=-1)

        a = conv_in[..., :CONV_CH] * jax.nn.sigmoid(conv_in[..., CONV_CH:])
        a = lax.conv_general_dilated(
            a, conv_w[l][:, None, :], window_strides=(1,), padding=[(CONV_WIDTH - 1, 0)],
            dimension_numbers=('NWC', 'WIO', 'NWC'), feature_group_count=CONV_CH) + conv_b[l]
        a = jax.nn.silu(layernorm(a, conv_ln_g[l], conv_ln_b[l]))
        y_a = a @ w_conv_out[l]

        q = (rmsnorm(cq, g_q_a[l]) @ w_uq[l]).reshape(B, S, MLA_HEADS, QK_DIM)
        kv = (rmsnorm(ckv, g_kv_a[l]) @ w_ukv[l]).reshape(B, S, MLA_HEADS, QK_NOPE + V_HEAD)
        k_nope, v = kv[..., :QK_NOPE], kv[..., QK_NOPE:]
        k_rot = jnp.broadcast_to(k_pe[:, :, None, :], (B, S, MLA_HEADS, QK_ROPE))
        k = jnp.concatenate([k_nope, k_rot], axis=-1)
        q = rmsnorm(q, g_qnorm[l])
        k = rmsnorm(k, g_knorm[l])
        q = jnp.concatenate([q[..., :QK_NOPE], apply_rope(q[..., QK_NOPE:], cos, sin)], axis=-1)
        k = jnp.concatenate([k[..., :QK_NOPE], apply_rope(k[..., QK_NOPE:], cos, sin)], axis=-1)
        o = causal_attention_blocked(q, k, v, mla_scale)
        y_b = o.reshape(B, S, MLA_HEADS * V_HEAD) @ w_mla_out[l]

        mk, mv = jnp.split(rmsnorm(mem, g_mem[l]) @ w_mem_kv[l], 2, axis=-1)
        mk = rmsnorm(mk.reshape(B, MEM_LEN, MEM_HEADS, MEM_HEAD_DIM), g_mknorm[l])
        mv = mv.reshape(B, MEM_LEN, MEM_HEADS, MEM_HEAD_DIM)
        mq = rmsnorm(mq.reshape(B, S, MEM_HEADS, MEM_HEAD_DIM), g_mqnorm[l])
        ms = jnp.einsum('bshd,bmhd->bhsm', mq, mk, preferred_element_type=jnp.float32) * mem_scale
        mp = jax.nn.softmax(ms, axis=-1).astype(mv.dtype)
        om = jnp.einsum('bhsm,bmhd->bshd', mp, mv).reshape(B, S, MEM_HEADS * MEM_HEAD_DIM)
        y_c = om @ w_mem_out[l]

        gates = jax.nn.sigmoid(gate_logits + b_gate[l]).reshape(B, S, N_BRANCH, D)
        merged = gates[:, :, 0] * y_a + gates[:, :, 1] * y_b + gates[:, :, 2] * y_c
        x = x + merged @ w_out[l]

        x = x + 0.5 * swiglu_ffn(rmsnorm(x, g_ffn2[l]), w_ffn2_gu[l], w_ffn2_down[l])
    return x
```

```python
import functools

import jax
import jax.numpy as jnp
from jax import lax
from jax.experimental import pallas as pl
from jax.experimental.pallas import tpu as pltpu

F32 = jnp.float32
BF16 = jnp.bfloat16

D_MODEL = 2048
MEM_LEN = 256
D_FF = 5632
CONV_CH = 1536
CONV_WIDTH = 31
MLA_HEADS = 12
Q_LORA = 512
KV_LORA = 512
QK_NOPE = 128
QK_ROPE = 64
QK_DIM = QK_NOPE + QK_ROPE
V_HEAD = 128
MEM_HEADS = 4
MEM_HEAD_DIM = 256
N_BRANCH = 3
ROPE_THETA = 10000.0
EPS = 1e-6
HALF_ROPE = QK_ROPE // 2

VMEM_LIMIT_BYTES = 56 * 1024 * 1024
CONV_HALO = 32


def _params(*sem):
    return pltpu.CompilerParams(dimension_semantics=sem, vmem_limit_bytes=VMEM_LIMIT_BYTES)


def _rms(x):
    return x * lax.rsqrt(jnp.mean(x * x, axis=-1, keepdims=True) + EPS)


def _dot(a, b):
    return jnp.dot(a, b, preferred_element_type=F32)


def _dot_nt(a, b):
    return lax.dot_general(a, b, (((1,), (1,)), ((), ())), preferred_element_type=F32)


def _ffn_body(x_ref, g_ref, wg_ref, wu_ref, wd_ref, gn_ref, o_ref, *rest, emit_norm):
    if emit_norm:
        h_ref, n_scr = rest
    else:
        (n_scr,) = rest
    j = pl.program_id(1)

    @pl.when(j == 0)
    def _():
        x = x_ref[...]
        n_scr[...] = (_rms(x) * g_ref[...]).astype(BF16)
        o_ref[...] = x

    n = n_scr[...]
    gg = _dot(n, wg_ref[...])
    uu = _dot(n, wu_ref[...])
    act = (gg * jax.nn.sigmoid(gg) * uu * 0.5).astype(BF16)
    o_ref[...] += _dot(act, wd_ref[...])

    if emit_norm:

        @pl.when(j == pl.num_programs(1) - 1)
        def _():
            h_ref[...] = (_rms(o_ref[...]) * gn_ref[...]).astype(BF16)


def _ffn(x, g, w_gu, w_down, g_next, *, emit_norm, tm, tf):
    S, D = x.shape
    nf = D_FF // tf
    out_shape = [jax.ShapeDtypeStruct((S, D), F32)]
    out_specs = [pl.BlockSpec((tm, D), lambda i, j: (i, 0))]
    if emit_norm:
        out_shape.append(jax.ShapeDtypeStruct((S, D), BF16))
        out_specs.append(pl.BlockSpec((tm, D), lambda i, j: (i, 0)))
    return pl.pallas_call(
        functools.partial(_ffn_body, emit_norm=emit_norm),
        grid=(S // tm, nf),
        in_specs=[
            pl.BlockSpec((tm, D), lambda i, j: (i, 0)),
            pl.BlockSpec((1, D), lambda i, j: (0, 0)),
            pl.BlockSpec((D, tf), lambda i, j: (0, j)),
            pl.BlockSpec((D, tf), lambda i, j: (0, j + nf)),
            pl.BlockSpec((tf, D), lambda i, j: (j, 0)),
            pl.BlockSpec((1, D), lambda i, j: (0, 0)),
        ],
        out_specs=out_specs,
        out_shape=out_shape,
        scratch_shapes=[pltpu.VMEM((tm, D), BF16)],
        compiler_params=_params("parallel", "arbitrary"),
        name="ffn_norm" if emit_norm else "ffn",
    )(x, g, w_gu, w_gu, w_down, g_next)


def _gates_body(h_ref, w_ref, b_ref, o_ref):
    o_ref[...] = jax.nn.sigmoid(_dot(h_ref[...], w_ref[...]) + b_ref[...]).astype(BF16)


def _gates(h, w, b, *, tm, tn):
    S, D = h.shape
    N = w.shape[1]
    return pl.pallas_call(
        _gates_body,
        grid=(S // tm, N // tn),
        in_specs=[
            pl.BlockSpec((tm, D), lambda i, j: (i, 0)),
            pl.BlockSpec((D, tn), lambda i, j: (0, j)),
            pl.BlockSpec((1, tn), lambda i, j: (0, j)),
        ],
        out_specs=pl.BlockSpec((tm, tn), lambda i, j: (i, j)),
        out_shape=jax.ShapeDtypeStruct((S, N), BF16),
        compiler_params=_params("parallel", "arbitrary"),
        name="gates",
    )(h, w, b)


def _glu_body(h_ref, wa_ref, ws_ref, o_ref):
    h = h_ref[...]
    o_ref[...] = (_dot(h, wa_ref[...]) * jax.nn.sigmoid(_dot(h, ws_ref[...]))).astype(BF16)


def _glu(h, w, *, tm, tn):
    S, D = h.shape
    nn = CONV_CH // tn
    return pl.pallas_call(
        _glu_body,
        grid=(S // tm, nn),
        in_specs=[
            pl.BlockSpec((tm, D), lambda i, j: (i, 0)),
            pl.BlockSpec((D, tn), lambda i, j: (0, j)),
            pl.BlockSpec((D, tn), lambda i, j: (0, j + nn)),
        ],
        out_specs=pl.BlockSpec((tm, tn), lambda i, j: (i, j)),
        out_shape=jax.ShapeDtypeStruct((S, CONV_CH), BF16),
        compiler_params=_params("parallel", "arbitrary"),
        name="glu",
    )(h, w, w)


_SP_CQ = 0
_SP_CKV = _SP_CQ + Q_LORA
_SP_MQ = _SP_CKV + KV_LORA
_SP_P1 = _SP_MQ + MEM_HEADS * MEM_HEAD_DIM
_SP_P2 = _SP_P1 + 2 * QK_ROPE
_SP_COLS = _SP_P2 + 2 * QK_ROPE


def _small_proj_body(h_ref, w_ref, pos_ref, gqa_ref, gkva_ref, gmq_ref, freq_ref, gc_ref, gs_ref,
                     cq_ref, ckv_ref, mq_ref, kaux_ref):
    r = _dot(h_ref[...], w_ref[...])
    cq_ref[...] = (_rms(r[:, _SP_CQ:_SP_CKV]) * gqa_ref[...]).astype(BF16)
    ckv_ref[...] = (_rms(r[:, _SP_CKV:_SP_MQ]) * gkva_ref[...]).astype(BF16)
    for hd in range(MEM_HEADS):
        lo = hd * MEM_HEAD_DIM
        m = r[:, _SP_MQ + lo:_SP_MQ + lo + MEM_HEAD_DIM]
        mq_ref[:, lo:lo + MEM_HEAD_DIM] = (_rms(m) * gmq_ref[...] * (MEM_HEAD_DIM ** -0.5)).astype(BF16)
    ang = pos_ref[...].astype(F32) * freq_ref[...]
    kaux_ref[...] = (r[:, _SP_P1:_SP_P2] * (gc_ref[...] * jnp.cos(ang))
                     + r[:, _SP_P2:_SP_COLS] * (gs_ref[...] * jnp.sin(ang)))


def _small_proj(h, w, pos, gqa, gkva, gmq, freq, gc, gs, *, tm):
    S, D = h.shape
    row = lambda n: pl.BlockSpec((1, n), lambda i: (0, 0))
    return pl.pallas_call(
        _small_proj_body,
        grid=(S // tm,),
        in_specs=[
            pl.BlockSpec((tm, D), lambda i: (i, 0)),
            pl.BlockSpec((D, _SP_COLS), lambda i: (0, 0)),
            pl.BlockSpec((tm, 1), lambda i: (i, 0)),
            row(Q_LORA), row(KV_LORA), row(MEM_HEAD_DIM), row(2 * QK_ROPE), row(2 * QK_ROPE), row(2 * QK_ROPE),
        ],
        out_specs=[
            pl.BlockSpec((tm, Q_LORA), lambda i: (i, 0)),
            pl.BlockSpec((tm, KV_LORA), lambda i: (i, 0)),
            pl.BlockSpec((tm, MEM_HEADS * MEM_HEAD_DIM), lambda i: (i, 0)),
            pl.BlockSpec((tm, 2 * QK_ROPE), lambda i: (i, 0)),
        ],
        out_shape=[
            jax.ShapeDtypeStruct((S, Q_LORA), BF16),
            jax.ShapeDtypeStruct((S, KV_LORA), BF16),
            jax.ShapeDtypeStruct((S, MEM_HEADS * MEM_HEAD_DIM), BF16),
            jax.ShapeDtypeStruct((S, 2 * QK_ROPE), F32),
        ],
        compiler_params=_params("parallel"),
        name="small_proj",
    )(h, w, pos, gqa, gkva, gmq, freq, gc, gs)


def _qkv_body(cq_ref, ckv_ref, kaux_ref, pos_ref, wqm_ref, wqs_ref, wkv_ref,
              gqn_ref, freq_ref, gqc_ref, gqs_ref, gkn_ref,
              q_ref, k_ref, v_ref, gc_scr, gs_scr, kss_scr):
    @pl.when(pl.program_id(1) == 0)
    def _():
        ang = pos_ref[...].astype(F32) * freq_ref[...]
        gc_scr[...] = gqc_ref[...] * jnp.cos(ang)
        gs_scr[...] = gqs_ref[...] * jnp.sin(ang)
        kaux = kaux_ref[...]
        lane = lax.broadcasted_iota(jnp.int32, kaux.shape, 1)
        kpe = jnp.where(lane >= QK_ROPE, kaux, 0.0)
        kss_scr[...] = jnp.sum(kpe * kpe, axis=-1, keepdims=True)

    cq = cq_ref[...]
    qm = _dot(cq, wqm_ref[0])
    qs = _dot(cq, wqs_ref[0])
    rq = lax.rsqrt(jnp.sum(qm * qm, axis=-1, keepdims=True) * (1.0 / QK_DIM) + EPS) * (QK_DIM ** -0.5)
    q_ref[0, :, 0:QK_NOPE] = (qm[:, 0:QK_NOPE] * rq * gqn_ref[...]).astype(BF16)
    q_ref[0, :, QK_NOPE:QK_DIM] = ((qm[:, QK_NOPE:QK_DIM] * gc_scr[...] + qs * gs_scr[...]) * rq).astype(BF16)

    kv = _dot(ckv_ref[...], wkv_ref[0])
    kn = kv[:, 0:QK_NOPE]
    rk = lax.rsqrt((jnp.sum(kn * kn, axis=-1, keepdims=True) + kss_scr[...]) * (1.0 / QK_DIM) + EPS)
    k_ref[0, :, 0:QK_NOPE] = (kn * rk * gkn_ref[...]).astype(BF16)
    k_ref[0, :, QK_NOPE:QK_DIM] = (kaux_ref[:, 0:QK_ROPE] * rk).astype(BF16)
    v_ref[0] = kv[:, QK_NOPE:QK_NOPE + V_HEAD].astype(BF16)


def _qkv_prep(cq, ckv, kaux, pos, wqm, wqs, wkv, gqn, freq, gqc, gqs, gkn, *, tm):
    S = cq.shape[0]
    H = MLA_HEADS
    row = lambda n: pl.BlockSpec((1, n), lambda i, h: (0, 0))
    return pl.pallas_call(
        _qkv_body,
        grid=(S // tm, H),
        in_specs=[
            pl.BlockSpec((tm, Q_LORA), lambda i, h: (i, 0)),
            pl.BlockSpec((tm, KV_LORA), lambda i, h: (i, 0)),
            pl.BlockSpec((tm, 2 * QK_ROPE), lambda i, h: (i, 0)),
            pl.BlockSpec((tm, 1), lambda i, h: (i, 0)),
            pl.BlockSpec((1, Q_LORA, QK_DIM), lambda i, h: (h, 0, 0)),
            pl.BlockSpec((1, Q_LORA, QK_ROPE), lambda i, h: (h, 0, 0)),
            pl.BlockSpec((1, KV_LORA, QK_NOPE + V_HEAD), lambda i, h: (h, 0, 0)),
            row(QK_NOPE), row(QK_ROPE), row(QK_ROPE), row(QK_ROPE), row(QK_NOPE),
        ],
        out_specs=[
            pl.BlockSpec((1, tm, QK_DIM), lambda i, h: (h, i, 0)),
            pl.BlockSpec((1, tm, QK_DIM), lambda i, h: (h, i, 0)),
            pl.BlockSpec((1, tm, V_HEAD), lambda i, h: (h, i, 0)),
        ],
        out_shape=[
            jax.ShapeDtypeStruct((H, S, QK_DIM), BF16),
            jax.ShapeDtypeStruct((H, S, QK_DIM), BF16),
            jax.ShapeDtypeStruct((H, S, V_HEAD), BF16),
        ],
        scratch_shapes=[pltpu.VMEM((tm, QK_ROPE), F32), pltpu.VMEM((tm, QK_ROPE), F32), pltpu.VMEM((tm, 1), F32)],
        compiler_params=_params("parallel", "arbitrary"),
        name="qkv_prep",
    )(cq, ckv, kaux, pos, wqm, wqs, wkv, gqn, freq, gqc, gqs, gkn)


def _attn_body(q_ref, k_ref, v_ref, o_ref, m_scr, l_scr, acc_scr, *, tq):
    qi = pl.program_id(1)
    q = q_ref[0]
    m_scr[...] = jnp.full(m_scr.shape, -jnp.inf, F32)
    l_scr[...] = jnp.zeros(l_scr.shape, F32)
    acc_scr[...] = jnp.zeros(acc_scr.shape, F32)

    def step(kb, masked):
        start = pl.multiple_of(kb * tq, tq)
        k = k_ref[0, pl.ds(start, tq), :]
        v = v_ref[0, pl.ds(start, tq), :]
        s = _dot_nt(q, k)
        if masked:
            row = lax.broadcasted_iota(jnp.int32, s.shape, 0)
            col = lax.broadcasted_iota(jnp.int32, s.shape, 1)
            s = jnp.where(col <= row, s, -jnp.inf)
        m_old = m_scr[...]
        m_new = jnp.maximum(m_old, jnp.max(s, axis=-1, keepdims=True))
        p = jnp.exp(s - m_new)
        alpha = jnp.exp(m_old - m_new)
        l_scr[...] = alpha * l_scr[...] + jnp.sum(p, axis=-1, keepdims=True)
        acc_scr[...] = alpha * acc_scr[...] + _dot(p.astype(BF16), v)
        m_scr[...] = m_new

    def body(kb, carry):
        step(kb, False)
        return carry

    lax.fori_loop(0, qi, body, 0)
    step(qi, True)
    o_ref[...] = (acc_scr[...] / l_scr[...]).astype(BF16)


def _attention(q, k, v, *, tq):
    H, S, _ = q.shape
    return pl.pallas_call(
        functools.partial(_attn_body, tq=tq),
        grid=(H, S // tq),
        in_specs=[
            pl.BlockSpec((1, tq, QK_DIM), lambda h, i: (h, i, 0)),
            pl.BlockSpec((1, S, QK_DIM), lambda h, i: (h, 0, 0)),
            pl.BlockSpec((1, S, V_HEAD), lambda h, i: (h, 0, 0)),
        ],
        out_specs=pl.BlockSpec((tq, V_HEAD), lambda h, i: (i, h)),
        out_shape=jax.ShapeDtypeStruct((S, H * V_HEAD), BF16),
        scratch_shapes=[pltpu.VMEM((tq, 1), F32), pltpu.VMEM((tq, 1), F32), pltpu.VMEM((tq, V_HEAD), F32)],
        compiler_params=_params("parallel", "arbitrary"),
        name="attention",
    )(q, k, v)


def _conv_body(cur_ref, halo_ref, w_ref, b_ref, lg_ref, lb_ref, o_ref, ext_scr, conv_scr, *, tm, rb):
    halo = halo_ref[...].astype(F32)
    ext_scr[0:CONV_HALO, :] = jnp.where(pl.program_id(0) == 0, 0.0, halo)
    ext_scr[CONV_HALO:CONV_HALO + tm, :] = cur_ref[...].astype(F32)
    base = CONV_HALO - (CONV_WIDTH - 1)
    for c0 in range(0, CONV_CH, 128):
        for r0 in range(0, tm, rb):
            acc = jnp.broadcast_to(b_ref[:, c0:c0 + 128], (rb, 128))
            for j in range(CONV_WIDTH):
                lo = r0 + base + j
                acc = acc + ext_scr[lo:lo + rb, c0:c0 + 128] * w_ref[j:j + 1, c0:c0 + 128]
            conv_scr[r0:r0 + rb, c0:c0 + 128] = acc
    a = conv_scr[...]
    xc = a - jnp.mean(a, axis=-1, keepdims=True)
    y = xc * lax.rsqrt(jnp.mean(xc * xc, axis=-1, keepdims=True) + EPS) * lg_ref[...] + lb_ref[...]
    o_ref[...] = (y * jax.nn.sigmoid(y)).astype(BF16)


def _conv(a, w, b, lg, lb, *, tm, rb):
    S, C = a.shape
    hb = tm // CONV_HALO
    row = lambda: pl.BlockSpec((1, C), lambda i: (0, 0))
    return pl.pallas_call(
        functools.partial(_conv_body, tm=tm, rb=rb),
        grid=(S // tm,),
        in_specs=[
            pl.BlockSpec((tm, C), lambda i: (i, 0)),
            pl.BlockSpec((CONV_HALO, C), lambda i: (jnp.maximum(i * hb - 1, 0), 0)),
            pl.BlockSpec((CONV_WIDTH, C), lambda i: (0, 0)),
            row(), row(), row(),
        ],
        out_specs=pl.BlockSpec((tm, C), lambda i: (i, 0)),
        out_shape=jax.ShapeDtypeStruct((S, C), BF16),
        scratch_shapes=[pltpu.VMEM((CONV_HALO + tm, C), F32), pltpu.VMEM((tm, C), F32)],
        compiler_params=_params("parallel"),
        name="conv",
    )(a, a, w, b, lg, lb)


def _mem_kv_body(mem_ref, g_ref, w_ref, gk_ref, mk_ref, mv_ref):
    n = (_rms(mem_ref[...]) * g_ref[...]).astype(BF16)
    kv = _dot(n, w_ref[...])
    width = MEM_HEADS * MEM_HEAD_DIM
    for hd in range(MEM_HEADS):
        lo = hd * MEM_HEAD_DIM
        mk_ref[:, lo:lo + MEM_HEAD_DIM] = (_rms(kv[:, lo:lo + MEM_HEAD_DIM]) * gk_ref[...]).astype(BF16)
    mv_ref[...] = kv[:, width:2 * width].astype(BF16)


def _mem_kv(mem, g, w, gk):
    M, D = mem.shape
    width = MEM_HEADS * MEM_HEAD_DIM
    full = lambda shape: pl.BlockSpec(shape, lambda i: (0,) * len(shape))
    return pl.pallas_call(
        _mem_kv_body,
        grid=(1,),
        in_specs=[full((M, D)), full((1, D)), full((D, 2 * width)), full((1, MEM_HEAD_DIM))],
        out_specs=[full((M, width)), full((M, width))],
        out_shape=[jax.ShapeDtypeStruct((M, width), BF16), jax.ShapeDtypeStruct((M, width), BF16)],
        compiler_params=_params("arbitrary"),
        name="mem_kv",
    )(mem, g, w, gk)


def _mem_attn_body(mq_ref, mk_ref, mv_ref, o_ref):
    for hd in range(MEM_HEADS):
        lo = hd * MEM_HEAD_DIM
        s = _dot_nt(mq_ref[:, lo:lo + MEM_HEAD_DIM], mk_ref[:, lo:lo + MEM_HEAD_DIM])
        p = jnp.exp(s - jnp.max(s, axis=-1, keepdims=True))
        p = p / jnp.sum(p, axis=-1, keepdims=True)
        o_ref[:, lo:lo + MEM_HEAD_DIM] = _dot(p.astype(BF16), mv_ref[:, lo:lo + MEM_HEAD_DIM]).astype(BF16)


def _mem_attn(mq, mk, mv, *, tm):
    S, W = mq.shape
    M = mk.shape[0]
    return pl.pallas_call(
        _mem_attn_body,
        grid=(S // tm,),
        in_specs=[
            pl.BlockSpec((tm, W), lambda i: (i, 0)),
            pl.BlockSpec((M, W), lambda i: (0, 0)),
            pl.BlockSpec((M, W), lambda i: (0, 0)),
        ],
        out_specs=pl.BlockSpec((tm, W), lambda i: (i, 0)),
        out_shape=jax.ShapeDtypeStruct((S, W), BF16),
        compiler_params=_params("parallel"),
        name="mem_attn",
    )(mq, mk, mv)


def _merge_body(x_ref, c_ref, o_ref, om_ref, g0_ref, g1_ref, g2_ref, wc_ref, wo_ref, wm_ref, wout_ref, out_ref):
    @pl.when(pl.program_id(1) == 0)
    def _():
        out_ref[...] = x_ref[...]

    merged = (g0_ref[...].astype(F32) * _dot(c_ref[...], wc_ref[...])
              + g1_ref[...].astype(F32) * _dot(o_ref[...], wo_ref[...])
              + g2_ref[...].astype(F32) * _dot(om_ref[...], wm_ref[...]))
    out_ref[...] += _dot(merged.astype(BF16), wout_ref[...])


def _merge(x, c, o, om, gates, wc, wo, wm, wout, *, tm, tn):
    S, D = x.shape
    nn = D // tn
    return pl.pallas_call(
        _merge_body,
        grid=(S // tm, nn),
        in_specs=[
            pl.BlockSpec((tm, D), lambda i, j: (i, 0)),
            pl.BlockSpec((tm, c.shape[1]), lambda i, j: (i, 0)),
            pl.BlockSpec((tm, o.shape[1]), lambda i, j: (i, 0)),
            pl.BlockSpec((tm, om.shape[1]), lambda i, j: (i, 0)),
            pl.BlockSpec((tm, tn), lambda i, j: (i, j)),
            pl.BlockSpec((tm, tn), lambda i, j: (i, j + nn)),
            pl.BlockSpec((tm, tn), lambda i, j: (i, j + 2 * nn)),
            pl.BlockSpec((wc.shape[0], tn), lambda i, j: (0, j)),
            pl.BlockSpec((wo.shape[0], tn), lambda i, j: (0, j)),
            pl.BlockSpec((wm.shape[0], tn), lambda i, j: (0, j)),
            pl.BlockSpec((tn, D), lambda i, j: (j, 0)),
        ],
        out_specs=pl.BlockSpec((tm, D), lambda i, j: (i, 0)),
        out_shape=jax.ShapeDtypeStruct((S, D), F32),
        compiler_params=_params("parallel", "arbitrary"),
        name="merge",
    )(x, c, o, om, gates, gates, gates, wc, wo, wm, wout)


def _swap_halves(v):
    return jnp.concatenate([v[..., HALF_ROPE:], v[..., :HALF_ROPE]], axis=-1)


def _tile(S, want):
    return min(S, want)


def _layer(x, mem, pos, p, l):
    S = x.shape[0]
    row = lambda v: v.reshape(1, -1).astype(F32)
    bf = lambda w: w.astype(BF16)

    inv_freq = ROPE_THETA ** (-jnp.arange(0, QK_ROPE, 2, dtype=F32) / QK_ROPE)
    freq64 = jnp.concatenate([inv_freq, inv_freq]).reshape(1, QK_ROPE)
    sign64 = jnp.concatenate([-jnp.ones((HALF_ROPE,), F32), jnp.ones((HALF_ROPE,), F32)])
    zeros64 = jnp.zeros((QK_ROPE,), F32)

    w_in = p["w_in"][l]
    c0 = 2 * CONV_CH
    c1 = c0 + Q_LORA
    c2 = c1 + KV_LORA
    c3 = c2 + QK_ROPE
    c4 = c3 + MEM_HEADS * MEM_HEAD_DIM
    w_kpe = w_in[:, c2:c3]
    w_small = bf(jnp.concatenate(
        [w_in[:, c0:c1], w_in[:, c1:c2], w_in[:, c3:c4], w_kpe, w_kpe, _swap_halves(w_kpe), jnp.zeros_like(w_kpe)],
        axis=1))
    w_glu = bf(w_in[:, :c0])
    w_gate = bf(w_in[:, c4:])

    w_uq = p["w_uq"][l].reshape(Q_LORA, MLA_HEADS, QK_DIM).transpose(1, 0, 2)
    wq_main = bf(w_uq)
    wq_swap = bf(_swap_halves(w_uq[..., QK_NOPE:]))
    w_kv = bf(p["w_ukv"][l].reshape(KV_LORA, MLA_HEADS, QK_NOPE + V_HEAD).transpose(1, 0, 2))

    g_q = p["g_qnorm"][l]
    g_k = p["g_knorm"][l]
    gq_rope, gk_rope = g_q[QK_NOPE:], g_k[QK_NOPE:]
    k_gc = jnp.concatenate([gk_rope, jnp.ones((QK_ROPE,), F32)])
    k_gs = jnp.concatenate([sign64 * _swap_halves(gk_rope), zeros64])
    k_freq = jnp.concatenate([freq64[0], zeros64])

    x1, h = _ffn(x, row(p["g_ffn1"][l]), bf(p["w_ffn1_gu"][l]), bf(p["w_ffn1_down"][l]), row(p["g_mix"][l]),
                 emit_norm=True, tm=_tile(S, 512), tf=512)

    gates = _gates(h, w_gate, row(p["b_gate"][l]), tm=_tile(S, 1024), tn=1024)
    a = _glu(h, w_glu, tm=_tile(S, 1024), tn=512)
    cq, ckv, mq, kaux = _small_proj(
        h, w_small, pos, row(p["g_q_a"][l]), row(p["g_kv_a"][l]), row(p["g_mqnorm"][l]),
        row(k_freq), row(k_gc), row(k_gs), tm=_tile(S, 512))

    conv = _conv(a, p["conv_w"][l].astype(F32), row(p["conv_b"][l]), row(p["conv_ln_g"][l]),
                 row(p["conv_ln_b"][l]), tm=_tile(S, 256), rb=64)

    q, k, v = _qkv_prep(cq, ckv, kaux, pos, wq_main, wq_swap, w_kv, row(g_q[:QK_NOPE]), freq64,
                        row(gq_rope), row(sign64 * _swap_halves(gq_rope)), row(g_k[:QK_NOPE]), tm=_tile(S, 512))
    o = _attention(q, k, v, tq=_tile(S, 512))

    mk, mv = _mem_kv(mem, row(p["g_mem"][l]), bf(p["w_mem_kv"][l]), row(p["g_mknorm"][l]))
    om = _mem_attn(mq, mk, mv, tm=_tile(S, 512))

    x2 = _merge(x1, conv, o, om, gates, bf(p["w_conv_out"][l]), bf(p["w_mla_out"][l]), bf(p["w_mem_out"][l]),
                bf(p["w_out"][l]), tm=_tile(S, 512), tn=512)

    (x3,) = _ffn(x2, row(p["g_ffn2"][l]), bf(p["w_ffn2_gu"][l]), bf(p["w_ffn2_down"][l]), row(p["g_ffn2"][l]),
                 emit_norm=False, tm=_tile(S, 512), tf=512)
    return x3


def kernel(x, mem, positions, g_ffn1, w_ffn1_gu, w_ffn1_down, g_mix, w_in, b_gate, conv_w, conv_b, conv_ln_g, conv_ln_b, w_conv_out, g_q_a, w_uq, g_kv_a, w_ukv, g_qnorm, g_knorm, w_mla_out, g_mem, w_mem_kv, g_mqnorm, g_mknorm, w_mem_out, w_out, g_ffn2, w_ffn2_gu, w_ffn2_down):
    p = dict(g_ffn1=g_ffn1, w_ffn1_gu=w_ffn1_gu, w_ffn1_down=w_ffn1_down, g_mix=g_mix, w_in=w_in, b_gate=b_gate,
             conv_w=conv_w, conv_b=conv_b, conv_ln_g=conv_ln_g, conv_ln_b=conv_ln_b, w_conv_out=w_conv_out,
             g_q_a=g_q_a, w_uq=w_uq, g_kv_a=g_kv_a, w_ukv=w_ukv, g_qnorm=g_qnorm, g_knorm=g_knorm,
             w_mla_out=w_mla_out, g_mem=g_mem, w_mem_kv=w_mem_kv, g_mqnorm=g_mqnorm, g_mknorm=g_mknorm,
             w_mem_out=w_mem_out, w_out=w_out, g_ffn2=g_ffn2, w_ffn2_gu=w_ffn2_gu, w_ffn2_down=w_ffn2_down)
    B, S, D = x.shape
    depth = g_ffn1.shape[0]
    outs = []
    for b in range(B):
        xb = x[b]
        pos = positions[b].reshape(S, 1)
        for l in range(depth):
            xb = _layer(xb, mem[b], pos, p, l)
        outs.append(xb.reshape(1, S, D))
    return outs[0] if B == 1 else jnp.concatenate(outs, axis=0)
```

```python
import functools

import jax
import jax.numpy as jnp
from jax import lax
from jax.experimental import pallas as pl
from jax.experimental.pallas import tpu as pltpu

F32 = jnp.float32
BF16 = jnp.bfloat16

D_MODEL = 2048
MEM_LEN = 256
D_FF = 5632
CONV_CH = 1536
CONV_WIDTH = 31
MLA_HEADS = 12
Q_LORA = 512
KV_LORA = 512
QK_NOPE = 128
QK_ROPE = 64
QK_DIM = QK_NOPE + QK_ROPE
V_HEAD = 128
MEM_HEADS = 4
MEM_HEAD_DIM = 256
N_BRANCH = 3
ROPE_THETA = 10000.0
EPS = 1e-6
HALF_ROPE = QK_ROPE // 2
LOG2E = 1.4426950408889634
MAX_STABLE_BOUND_LOG2 = 30.0 * LOG2E

VMEM_LIMIT_BYTES = 56 * 1024 * 1024
CONV_HALO = 32


def _params(*sem):
    return pltpu.CompilerParams(dimension_semantics=sem, vmem_limit_bytes=VMEM_LIMIT_BYTES)


def _rms(x):
    return x * lax.rsqrt(jnp.mean(x * x, axis=-1, keepdims=True) + EPS)


def _dot(a, b):
    return jnp.dot(a, b, preferred_element_type=F32)


def _dot_nt(a, b):
    return lax.dot_general(a, b, (((1,), (1,)), ((), ())), preferred_element_type=F32)


def _ffn_body(x_ref, g_ref, wg_ref, wu_ref, wd_ref, gn_ref, o_ref, *rest, emit_norm):
    if emit_norm:
        h_ref, n_scr = rest
    else:
        (n_scr,) = rest
    j = pl.program_id(1)

    @pl.when(j == 0)
    def _():
        x = x_ref[...]
        n_scr[...] = (_rms(x) * g_ref[...]).astype(BF16)
        o_ref[...] = x

    n = n_scr[...]
    gg = _dot(n, wg_ref[...])
    uu = _dot(n, wu_ref[...])
    act = (gg * jax.nn.sigmoid(gg) * uu * 0.5).astype(BF16)
    o_ref[...] += _dot(act, wd_ref[...])

    if emit_norm:

        @pl.when(j == pl.num_programs(1) - 1)
        def _():
            h_ref[...] = (_rms(o_ref[...]) * gn_ref[...]).astype(BF16)


def _ffn(x, g, w_gu, w_down, g_next, *, emit_norm, tm, tf):
    S, D = x.shape
    nf = D_FF // tf
    out_shape = [jax.ShapeDtypeStruct((S, D), F32)]
    out_specs = [pl.BlockSpec((tm, D), lambda i, j: (i, 0))]
    if emit_norm:
        out_shape.append(jax.ShapeDtypeStruct((S, D), BF16))
        out_specs.append(pl.BlockSpec((tm, D), lambda i, j: (i, 0)))
    return pl.pallas_call(
        functools.partial(_ffn_body, emit_norm=emit_norm),
        grid=(S // tm, nf),
        in_specs=[
            pl.BlockSpec((tm, D), lambda i, j: (i, 0)),
            pl.BlockSpec((1, D), lambda i, j: (0, 0)),
            pl.BlockSpec((D, tf), lambda i, j: (0, j)),
            pl.BlockSpec((D, tf), lambda i, j: (0, j + nf)),
            pl.BlockSpec((tf, D), lambda i, j: (j, 0)),
            pl.BlockSpec((1, D), lambda i, j: (0, 0)),
        ],
        out_specs=out_specs,
        out_shape=out_shape,
        scratch_shapes=[pltpu.VMEM((tm, D), BF16)],
        compiler_params=_params("parallel", "arbitrary"),
        name="ffn_norm" if emit_norm else "ffn",
    )(x, g, w_gu, w_gu, w_down, g_next)


def _gates_body(h_ref, w_ref, b_ref, o_ref):
    o_ref[...] = jax.nn.sigmoid(_dot(h_ref[...], w_ref[...]) + b_ref[...]).astype(BF16)


def _gates(h, w, b, *, tm, tn):
    S, D = h.shape
    N = w.shape[1]
    return pl.pallas_call(
        _gates_body,
        grid=(S // tm, N // tn),
        in_specs=[
            pl.BlockSpec((tm, D), lambda i, j: (i, 0)),
            pl.BlockSpec((D, tn), lambda i, j: (0, j)),
            pl.BlockSpec((1, tn), lambda i, j: (0, j)),
        ],
        out_specs=pl.BlockSpec((tm, tn), lambda i, j: (i, j)),
        out_shape=jax.ShapeDtypeStruct((S, N), BF16),
        compiler_params=_params("parallel", "arbitrary"),
        name="gates",
    )(h, w, b)


def _glu_body(h_ref, wa_ref, ws_ref, o_ref):
    h = h_ref[...]
    o_ref[...] = (_dot(h, wa_ref[...]) * jax.nn.sigmoid(_dot(h, ws_ref[...]))).astype(BF16)


def _glu(h, w, *, tm, tn):
    S, D = h.shape
    nn = CONV_CH // tn
    return pl.pallas_call(
        _glu_body,
        grid=(S // tm, nn),
        in_specs=[
            pl.BlockSpec((tm, D), lambda i, j: (i, 0)),
            pl.BlockSpec((D, tn), lambda i, j: (0, j)),
            pl.BlockSpec((D, tn), lambda i, j: (0, j + nn)),
        ],
        out_specs=pl.BlockSpec((tm, tn), lambda i, j: (i, j)),
        out_shape=jax.ShapeDtypeStruct((S, CONV_CH), BF16),
        compiler_params=_params("parallel", "arbitrary"),
        name="glu",
    )(h, w, w)


_SP_CQ = 0
_SP_CKV = _SP_CQ + Q_LORA
_SP_MQ = _SP_CKV + KV_LORA
_SP_P1 = _SP_MQ + MEM_HEADS * MEM_HEAD_DIM
_SP_P2 = _SP_P1 + 2 * QK_ROPE
_SP_COLS = _SP_P2 + 2 * QK_ROPE


def _small_proj_body(h_ref, w_ref, pos_ref, gqa_ref, gkva_ref, gmq_ref, freq_ref, gc_ref, gs_ref,
                     cq_ref, ckv_ref, mq_ref, kaux_ref):
    r = _dot(h_ref[...], w_ref[...])
    cq_ref[...] = (_rms(r[:, _SP_CQ:_SP_CKV]) * gqa_ref[...]).astype(BF16)
    ckv_ref[...] = (_rms(r[:, _SP_CKV:_SP_MQ]) * gkva_ref[...]).astype(BF16)
    for hd in range(MEM_HEADS):
        lo = hd * MEM_HEAD_DIM
        m = r[:, _SP_MQ + lo:_SP_MQ + lo + MEM_HEAD_DIM]
        mq_ref[:, lo:lo + MEM_HEAD_DIM] = (_rms(m) * gmq_ref[...] * (MEM_HEAD_DIM ** -0.5)).astype(BF16)
    ang = pos_ref[...].astype(F32) * freq_ref[...]
    kaux_ref[...] = (r[:, _SP_P1:_SP_P2] * (gc_ref[...] * jnp.cos(ang))
                     + r[:, _SP_P2:_SP_COLS] * (gs_ref[...] * jnp.sin(ang)))


def _small_proj(h, w, pos, gqa, gkva, gmq, freq, gc, gs, *, tm):
    S, D = h.shape
    row = lambda n: pl.BlockSpec((1, n), lambda i: (0, 0))
    return pl.pallas_call(
        _small_proj_body,
        grid=(S // tm,),
        in_specs=[
            pl.BlockSpec((tm, D), lambda i: (i, 0)),
            pl.BlockSpec((D, _SP_COLS), lambda i: (0, 0)),
            pl.BlockSpec((tm, 1), lambda i: (i, 0)),
            row(Q_LORA), row(KV_LORA), row(MEM_HEAD_DIM), row(2 * QK_ROPE), row(2 * QK_ROPE), row(2 * QK_ROPE),
        ],
        out_specs=[
            pl.BlockSpec((tm, Q_LORA), lambda i: (i, 0)),
            pl.BlockSpec((tm, KV_LORA), lambda i: (i, 0)),
            pl.BlockSpec((tm, MEM_HEADS * MEM_HEAD_DIM), lambda i: (i, 0)),
            pl.BlockSpec((tm, 2 * QK_ROPE), lambda i: (i, 0)),
        ],
        out_shape=[
            jax.ShapeDtypeStruct((S, Q_LORA), BF16),
            jax.ShapeDtypeStruct((S, KV_LORA), BF16),
            jax.ShapeDtypeStruct((S, MEM_HEADS * MEM_HEAD_DIM), BF16),
            jax.ShapeDtypeStruct((S, 2 * QK_ROPE), F32),
        ],
        compiler_params=_params("parallel"),
        name="small_proj",
    )(h, w, pos, gqa, gkva, gmq, freq, gc, gs)


def _qkv_body(cq_ref, ckv_ref, kaux_ref, pos_ref, wqm_ref, wqs_ref, wkv_ref,
              gqn_ref, freq_ref, gqc_ref, gqs_ref, gkn_ref,
              q_ref, k_ref, v_ref, gc_scr, gs_scr, kss_scr):
    @pl.when(pl.program_id(1) == 0)
    def _():
        ang = pos_ref[...].astype(F32) * freq_ref[...]
        gc_scr[...] = gqc_ref[...] * jnp.cos(ang)
        gs_scr[...] = gqs_ref[...] * jnp.sin(ang)
        kaux = kaux_ref[...]
        lane = lax.broadcasted_iota(jnp.int32, kaux.shape, 1)
        kpe = jnp.where(lane >= QK_ROPE, kaux, 0.0)
        kss_scr[...] = jnp.sum(kpe * kpe, axis=-1, keepdims=True)

    cq = cq_ref[...]
    qm = _dot(cq, wqm_ref[0])
    qs = _dot(cq, wqs_ref[0])
    rq = lax.rsqrt(jnp.sum(qm * qm, axis=-1, keepdims=True) * (1.0 / QK_DIM) + EPS) * (QK_DIM ** -0.5 * LOG2E)
    q_ref[0, :, 0:QK_NOPE] = (qm[:, 0:QK_NOPE] * rq * gqn_ref[...]).astype(BF16)
    q_ref[0, :, QK_NOPE:QK_DIM] = ((qm[:, QK_NOPE:QK_DIM] * gc_scr[...] + qs * gs_scr[...]) * rq).astype(BF16)

    kv = _dot(ckv_ref[...], wkv_ref[0])
    kn = kv[:, 0:QK_NOPE]
    rk = lax.rsqrt((jnp.sum(kn * kn, axis=-1, keepdims=True) + kss_scr[...]) * (1.0 / QK_DIM) + EPS)
    k_ref[0, :, 0:QK_NOPE] = (kn * rk * gkn_ref[...]).astype(BF16)
    k_ref[0, :, QK_NOPE:QK_DIM] = (kaux_ref[:, 0:QK_ROPE] * rk).astype(BF16)
    v_ref[0] = kv[:, QK_NOPE:QK_NOPE + V_HEAD].astype(BF16)


def _qkv_prep(cq, ckv, kaux, pos, wqm, wqs, wkv, gqn, freq, gqc, gqs, gkn, *, tm):
    S = cq.shape[0]
    H = MLA_HEADS
    row = lambda n: pl.BlockSpec((1, n), lambda i, h: (0, 0))
    return pl.pallas_call(
        _qkv_body,
        grid=(S // tm, H),
        in_specs=[
            pl.BlockSpec((tm, Q_LORA), lambda i, h: (i, 0)),
            pl.BlockSpec((tm, KV_LORA), lambda i, h: (i, 0)),
            pl.BlockSpec((tm, 2 * QK_ROPE), lambda i, h: (i, 0)),
            pl.BlockSpec((tm, 1), lambda i, h: (i, 0)),
            pl.BlockSpec((1, Q_LORA, QK_DIM), lambda i, h: (h, 0, 0)),
            pl.BlockSpec((1, Q_LORA, QK_ROPE), lambda i, h: (h, 0, 0)),
            pl.BlockSpec((1, KV_LORA, QK_NOPE + V_HEAD), lambda i, h: (h, 0, 0)),
            row(QK_NOPE), row(QK_ROPE), row(QK_ROPE), row(QK_ROPE), row(QK_NOPE),
        ],
        out_specs=[
            pl.BlockSpec((1, tm, QK_DIM), lambda i, h: (h, i, 0)),
            pl.BlockSpec((1, tm, QK_DIM), lambda i, h: (h, i, 0)),
            pl.BlockSpec((1, tm, V_HEAD), lambda i, h: (h, i, 0)),
        ],
        out_shape=[
            jax.ShapeDtypeStruct((H, S, QK_DIM), BF16),
            jax.ShapeDtypeStruct((H, S, QK_DIM), BF16),
            jax.ShapeDtypeStruct((H, S, V_HEAD), BF16),
        ],
        scratch_shapes=[pltpu.VMEM((tm, QK_ROPE), F32), pltpu.VMEM((tm, QK_ROPE), F32), pltpu.VMEM((tm, 1), F32)],
        compiler_params=_params("parallel", "arbitrary"),
        name="qkv_prep",
    )(cq, ckv, kaux, pos, wqm, wqs, wkv, gqn, freq, gqc, gqs, gkn)


def _attn_body(sc_ref, q_ref, k_ref, v_ref, o_ref, l_scr, acc_scr, m_scr, *, tq, unroll):
    qi = pl.program_id(1)
    q = q_ref[0]

    def kv_block(kb):
        start = pl.multiple_of(kb * tq, tq)
        return k_ref[0, pl.ds(start, tq), :], v_ref[0, pl.ds(start, tq), :]

    def causal(x, fill):
        row = lax.broadcasted_iota(jnp.int32, x.shape, 0)
        col = lax.broadcasted_iota(jnp.int32, x.shape, 1)
        return jnp.where(col <= row, x, fill)

    @pl.when(sc_ref[1] > 0.0)
    def _bounded():
        bound = sc_ref[0]

        def block(kb, masked):
            k, v = kv_block(kb)
            p = jnp.exp2(_dot_nt(q, k) - bound)
            if masked:
                p = causal(p, 0.0)
            lanes = p[:, 0:128]
            for c in range(128, tq, 128):
                lanes = lanes + p[:, c:c + 128]
            return lanes, _dot(p.astype(BF16), v)

        def add(kb, n, masked=False):
            lanes, pv = block(kb, masked)
            for u in range(1, n):
                lanes_u, pv_u = block(kb + u, masked)
                lanes, pv = lanes + lanes_u, pv + pv_u
            return lanes, pv

        l_scr[...] = jnp.zeros(l_scr.shape, F32)
        acc_scr[...] = jnp.zeros(acc_scr.shape, F32)

        def many(sb, carry):
            lanes, pv = add(sb * unroll, unroll)
            l_scr[...] += lanes
            acc_scr[...] += pv
            return carry

        def one(kb, carry):
            lanes, pv = add(kb, 1)
            l_scr[...] += lanes
            acc_scr[...] += pv
            return carry

        n_many = qi // unroll
        lax.fori_loop(0, n_many, many, 0)
        lax.fori_loop(n_many * unroll, qi, one, 0)
        lanes, pv = add(qi, 1, masked=True)
        l = jnp.sum(l_scr[...] + lanes, axis=-1, keepdims=True)
        o_ref[...] = ((acc_scr[...] + pv) / l).astype(BF16)

    @pl.when(sc_ref[1] <= 0.0)
    def _online():
        m_scr[...] = jnp.full(m_scr.shape, -jnp.inf, F32)
        l_scr[...] = jnp.zeros(l_scr.shape, F32)
        acc_scr[...] = jnp.zeros(acc_scr.shape, F32)

        def step(kb, masked):
            k, v = kv_block(kb)
            s = _dot_nt(q, k)
            if masked:
                s = causal(s, -jnp.inf)
            m_old = m_scr[...]
            m_new = jnp.maximum(m_old, jnp.max(s, axis=-1, keepdims=True))
            p = jnp.exp2(s - m_new)
            alpha = jnp.exp2(m_old - m_new)
            l_scr[:, 0:1] = alpha * l_scr[:, 0:1] + jnp.sum(p, axis=-1, keepdims=True)
            acc_scr[...] = alpha * acc_scr[...] + _dot(p.astype(BF16), v)
            m_scr[...] = m_new

        def body(kb, carry):
            step(kb, False)
            return carry

        lax.fori_loop(0, qi, body, 0)
        step(qi, True)
        o_ref[...] = (acc_scr[...] / l_scr[:, 0:1]).astype(BF16)


def _attention(sc, q, k, v, *, tq, unroll):
    H, S, _ = q.shape
    return pl.pallas_call(
        functools.partial(_attn_body, tq=tq, unroll=unroll),
        grid_spec=pltpu.PrefetchScalarGridSpec(
            num_scalar_prefetch=1,
            grid=(H, S // tq),
            in_specs=[
                pl.BlockSpec((1, tq, QK_DIM), lambda h, i, sc: (h, i, 0)),
                pl.BlockSpec((1, S, QK_DIM), lambda h, i, sc: (h, 0, 0)),
                pl.BlockSpec((1, S, V_HEAD), lambda h, i, sc: (h, 0, 0)),
            ],
            out_specs=pl.BlockSpec((tq, V_HEAD), lambda h, i, sc: (i, h)),
            scratch_shapes=[pltpu.VMEM((tq, 128), F32), pltpu.VMEM((tq, V_HEAD), F32), pltpu.VMEM((tq, 1), F32)],
        ),
        out_shape=jax.ShapeDtypeStruct((S, H * V_HEAD), BF16),
        compiler_params=_params("parallel", "arbitrary"),
        name="attention",
    )(sc, q, k, v)


def _conv_body(cur_ref, halo_ref, w_ref, b_ref, lg_ref, lb_ref, o_ref, ext_scr, conv_scr, *, tm, rb):
    halo = halo_ref[...].astype(F32)
    ext_scr[0:CONV_HALO, :] = jnp.where(pl.program_id(0) == 0, 0.0, halo)
    ext_scr[CONV_HALO:CONV_HALO + tm, :] = cur_ref[...].astype(F32)
    base = CONV_HALO - (CONV_WIDTH - 1)
    for c0 in range(0, CONV_CH, 128):
        for r0 in range(0, tm, rb):
            acc = jnp.broadcast_to(b_ref[:, c0:c0 + 128], (rb, 128))
            for j in range(CONV_WIDTH):
                lo = r0 + base + j
                acc = acc + ext_scr[lo:lo + rb, c0:c0 + 128] * w_ref[j:j + 1, c0:c0 + 128]
            conv_scr[r0:r0 + rb, c0:c0 + 128] = acc
    a = conv_scr[...]
    xc = a - jnp.mean(a, axis=-1, keepdims=True)
    y = xc * lax.rsqrt(jnp.mean(xc * xc, axis=-1, keepdims=True) + EPS) * lg_ref[...] + lb_ref[...]
    o_ref[...] = (y * jax.nn.sigmoid(y)).astype(BF16)


def _conv(a, w, b, lg, lb, *, tm, rb):
    S, C = a.shape
    hb = tm // CONV_HALO
    row = lambda: pl.BlockSpec((1, C), lambda i: (0, 0))
    return pl.pallas_call(
        functools.partial(_conv_body, tm=tm, rb=rb),
        grid=(S // tm,),
        in_specs=[
            pl.BlockSpec((tm, C), lambda i: (i, 0)),
            pl.BlockSpec((CONV_HALO, C), lambda i: (jnp.maximum(i * hb - 1, 0), 0)),
            pl.BlockSpec((CONV_WIDTH, C), lambda i: (0, 0)),
            row(), row(), row(),
        ],
        out_specs=pl.BlockSpec((tm, C), lambda i: (i, 0)),
        out_shape=jax.ShapeDtypeStruct((S, C), BF16),
        scratch_shapes=[pltpu.VMEM((CONV_HALO + tm, C), F32), pltpu.VMEM((tm, C), F32)],
        compiler_params=_params("parallel"),
        name="conv",
    )(a, a, w, b, lg, lb)


def _mem_kv_body(mem_ref, g_ref, w_ref, gk_ref, mk_ref, mv_ref):
    n = (_rms(mem_ref[...]) * g_ref[...]).astype(BF16)
    kv = _dot(n, w_ref[...])
    width = MEM_HEADS * MEM_HEAD_DIM
    for hd in range(MEM_HEADS):
        lo = hd * MEM_HEAD_DIM
        mk_ref[:, lo:lo + MEM_HEAD_DIM] = (_rms(kv[:, lo:lo + MEM_HEAD_DIM]) * gk_ref[...]).astype(BF16)
    mv_ref[...] = kv[:, width:2 * width].astype(BF16)


def _mem_kv(mem, g, w, gk):
    M, D = mem.shape
    width = MEM_HEADS * MEM_HEAD_DIM
    full = lambda shape: pl.BlockSpec(shape, lambda i: (0,) * len(shape))
    return pl.pallas_call(
        _mem_kv_body,
        grid=(1,),
        in_specs=[full((M, D)), full((1, D)), full((D, 2 * width)), full((1, MEM_HEAD_DIM))],
        out_specs=[full((M, width)), full((M, width))],
        out_shape=[jax.ShapeDtypeStruct((M, width), BF16), jax.ShapeDtypeStruct((M, width), BF16)],
        compiler_params=_params("arbitrary"),
        name="mem_kv",
    )(mem, g, w, gk)


def _mem_attn_body(mq_ref, mk_ref, mv_ref, o_ref):
    for hd in range(MEM_HEADS):
        lo = hd * MEM_HEAD_DIM
        s = _dot_nt(mq_ref[:, lo:lo + MEM_HEAD_DIM], mk_ref[:, lo:lo + MEM_HEAD_DIM])
        p = jnp.exp(s - jnp.max(s, axis=-1, keepdims=True))
        p = p / jnp.sum(p, axis=-1, keepdims=True)
        o_ref[:, lo:lo + MEM_HEAD_DIM] = _dot(p.astype(BF16), mv_ref[:, lo:lo + MEM_HEAD_DIM]).astype(BF16)


def _mem_attn(mq, mk, mv, *, tm):
    S, W = mq.shape
    M = mk.shape[0]
    return pl.pallas_call(
        _mem_attn_body,
        grid=(S // tm,),
        in_specs=[
            pl.BlockSpec((tm, W), lambda i: (i, 0)),
            pl.BlockSpec((M, W), lambda i: (0, 0)),
            pl.BlockSpec((M, W), lambda i: (0, 0)),
        ],
        out_specs=pl.BlockSpec((tm, W), lambda i: (i, 0)),
        out_shape=jax.ShapeDtypeStruct((S, W), BF16),
        compiler_params=_params("parallel"),
        name="mem_attn",
    )(mq, mk, mv)


def _merge_body(x_ref, c_ref, o_ref, om_ref, g0_ref, g1_ref, g2_ref, wc_ref, wo_ref, wm_ref, wout_ref, out_ref):
    @pl.when(pl.program_id(1) == 0)
    def _():
        out_ref[...] = x_ref[...]

    merged = (g0_ref[...].astype(F32) * _dot(c_ref[...], wc_ref[...])
              + g1_ref[...].astype(F32) * _dot(o_ref[...], wo_ref[...])
              + g2_ref[...].astype(F32) * _dot(om_ref[...], wm_ref[...]))
    out_ref[...] += _dot(merged.astype(BF16), wout_ref[...])


def _merge(x, c, o, om, gates, wc, wo, wm, wout, *, tm, tn):
    S, D = x.shape
    nn = D // tn
    return pl.pallas_call(
        _merge_body,
        grid=(S // tm, nn),
        in_specs=[
            pl.BlockSpec((tm, D), lambda i, j: (i, 0)),
            pl.BlockSpec((tm, c.shape[1]), lambda i, j: (i, 0)),
            pl.BlockSpec((tm, o.shape[1]), lambda i, j: (i, 0)),
            pl.BlockSpec((tm, om.shape[1]), lambda i, j: (i, 0)),
            pl.BlockSpec((tm, tn), lambda i, j: (i, j)),
            pl.BlockSpec((tm, tn), lambda i, j: (i, j + nn)),
            pl.BlockSpec((tm, tn), lambda i, j: (i, j + 2 * nn)),
            pl.BlockSpec((wc.shape[0], tn), lambda i, j: (0, j)),
            pl.BlockSpec((wo.shape[0], tn), lambda i, j: (0, j)),
            pl.BlockSpec((wm.shape[0], tn), lambda i, j: (0, j)),
            pl.BlockSpec((tn, D), lambda i, j: (j, 0)),
        ],
        out_specs=pl.BlockSpec((tm, D), lambda i, j: (i, 0)),
        out_shape=jax.ShapeDtypeStruct((S, D), F32),
        compiler_params=_params("parallel", "arbitrary"),
        name="merge",
    )(x, c, o, om, gates, gates, gates, wc, wo, wm, wout)


def _swap_halves(v):
    return jnp.concatenate([v[..., HALF_ROPE:], v[..., :HALF_ROPE]], axis=-1)


def _tile(S, want):
    return min(S, want)


def _layer(x, mem, pos, p, l):
    S = x.shape[0]
    row = lambda v: v.reshape(1, -1).astype(F32)
    bf = lambda w: w.astype(BF16)

    inv_freq = ROPE_THETA ** (-jnp.arange(0, QK_ROPE, 2, dtype=F32) / QK_ROPE)
    freq64 = jnp.concatenate([inv_freq, inv_freq]).reshape(1, QK_ROPE)
    sign64 = jnp.concatenate([-jnp.ones((HALF_ROPE,), F32), jnp.ones((HALF_ROPE,), F32)])
    zeros64 = jnp.zeros((QK_ROPE,), F32)

    w_in = p["w_in"][l]
    c0 = 2 * CONV_CH
    c1 = c0 + Q_LORA
    c2 = c1 + KV_LORA
    c3 = c2 + QK_ROPE
    c4 = c3 + MEM_HEADS * MEM_HEAD_DIM
    w_kpe = w_in[:, c2:c3]
    w_small = bf(jnp.concatenate(
        [w_in[:, c0:c1], w_in[:, c1:c2], w_in[:, c3:c4], w_kpe, w_kpe, _swap_halves(w_kpe), jnp.zeros_like(w_kpe)],
        axis=1))
    w_glu = bf(w_in[:, :c0])
    w_gate = bf(w_in[:, c4:])

    w_uq = p["w_uq"][l].reshape(Q_LORA, MLA_HEADS, QK_DIM).transpose(1, 0, 2)
    wq_main = bf(w_uq)
    wq_swap = bf(_swap_halves(w_uq[..., QK_NOPE:]))
    w_kv = bf(p["w_ukv"][l].reshape(KV_LORA, MLA_HEADS, QK_NOPE + V_HEAD).transpose(1, 0, 2))

    g_q = p["g_qnorm"][l]
    g_k = p["g_knorm"][l]
    gq_rope, gk_rope = g_q[QK_NOPE:], g_k[QK_NOPE:]
    k_gc = jnp.concatenate([gk_rope, jnp.ones((QK_ROPE,), F32)])
    k_gs = jnp.concatenate([sign64 * _swap_halves(gk_rope), zeros64])
    k_freq = jnp.concatenate([freq64[0], zeros64])

    x1, h = _ffn(x, row(p["g_ffn1"][l]), bf(p["w_ffn1_gu"][l]), bf(p["w_ffn1_down"][l]), row(p["g_mix"][l]),
                 emit_norm=True, tm=_tile(S, 512), tf=512)

    gates = _gates(h, w_gate, row(p["b_gate"][l]), tm=_tile(S, 1024), tn=1024)
    a = _glu(h, w_glu, tm=_tile(S, 1024), tn=512)
    cq, ckv, mq, kaux = _small_proj(
        h, w_small, pos, row(p["g_q_a"][l]), row(p["g_kv_a"][l]), row(p["g_mqnorm"][l]),
        row(k_freq), row(k_gc), row(k_gs), tm=_tile(S, 512))

    conv = _conv(a, p["conv_w"][l].astype(F32), row(p["conv_b"][l]), row(p["conv_ln_g"][l]),
                 row(p["conv_ln_b"][l]), tm=_tile(S, 256), rb=64)

    q, k, v = _qkv_prep(cq, ckv, kaux, pos, wq_main, wq_swap, w_kv, row(g_q[:QK_NOPE]), freq64,
                        row(gq_rope), row(sign64 * _swap_halves(gq_rope)), row(g_k[:QK_NOPE]), tm=_tile(S, 512))
    bound = (QK_DIM ** 0.5 * LOG2E * 1.01) * jnp.max(jnp.abs(g_q)) * jnp.max(jnp.abs(g_k))
    sc = jnp.stack([bound, (bound <= MAX_STABLE_BOUND_LOG2).astype(F32)])
    o = _attention(sc, q, k, v, tq=_tile(S, 512), unroll=4)

    mk, mv = _mem_kv(mem, row(p["g_mem"][l]), bf(p["w_mem_kv"][l]), row(p["g_mknorm"][l]))
    om = _mem_attn(mq, mk, mv, tm=_tile(S, 512))

    x2 = _merge(x1, conv, o, om, gates, bf(p["w_conv_out"][l]), bf(p["w_mla_out"][l]), bf(p["w_mem_out"][l]),
                bf(p["w_out"][l]), tm=_tile(S, 512), tn=512)

    (x3,) = _ffn(x2, row(p["g_ffn2"][l]), bf(p["w_ffn2_gu"][l]), bf(p["w_ffn2_down"][l]), row(p["g_ffn2"][l]),
                 emit_norm=False, tm=_tile(S, 512), tf=512)
    return x3


def kernel(x, mem, positions, g_ffn1, w_ffn1_gu, w_ffn1_down, g_mix, w_in, b_gate, conv_w, conv_b, conv_ln_g, conv_ln_b, w_conv_out, g_q_a, w_uq, g_kv_a, w_ukv, g_qnorm, g_knorm, w_mla_out, g_mem, w_mem_kv, g_mqnorm, g_mknorm, w_mem_out, w_out, g_ffn2, w_ffn2_gu, w_ffn2_down):
    p = dict(g_ffn1=g_ffn1, w_ffn1_gu=w_ffn1_gu, w_ffn1_down=w_ffn1_down, g_mix=g_mix, w_in=w_in, b_gate=b_gate,
             conv_w=conv_w, conv_b=conv_b, conv_ln_g=conv_ln_g, conv_ln_b=conv_ln_b, w_conv_out=w_conv_out,
             g_q_a=g_q_a, w_uq=w_uq, g_kv_a=g_kv_a, w_ukv=w_ukv, g_qnorm=g_qnorm, g_knorm=g_knorm,
             w_mla_out=w_mla_out, g_mem=g_mem, w_mem_kv=w_mem_kv, g_mqnorm=g_mqnorm, g_mknorm=g_mknorm,
             w_mem_out=w_mem_out, w_out=w_out, g_ffn2=g_ffn2, w_ffn2_gu=w_ffn2_gu, w_ffn2_down=w_ffn2_down)
    B, S, D = x.shape
    depth = g_ffn1.shape[0]
    outs = []
    for b in range(B):
        xb = x[b]
        pos = positions[b].reshape(S, 1)
        for l in range(depth):
            xb = _layer(xb, mem[b], pos, p, l)
        outs.append(xb.reshape(1, S, D))
    return outs[0] if B == 1 else jnp.concatenate(outs, axis=0)
```

```python
import functools

import jax
import jax.numpy as jnp
from jax import lax
from jax.experimental import pallas as pl
from jax.experimental.pallas import tpu as pltpu

F32 = jnp.float32
BF16 = jnp.bfloat16

D_MODEL = 2048
MEM_LEN = 256
D_FF = 5632
CONV_CH = 1536
CONV_WIDTH = 31
MLA_HEADS = 12
Q_LORA = 512
KV_LORA = 512
QK_NOPE = 128
QK_ROPE = 64
QK_DIM = QK_NOPE + QK_ROPE
V_HEAD = 128
MEM_HEADS = 4
MEM_HEAD_DIM = 256
N_BRANCH = 3
ROPE_THETA = 10000.0
EPS = 1e-6
HALF_ROPE = QK_ROPE // 2
LOG2E = 1.4426950408889634
MAX_STABLE_BOUND_LOG2 = 30.0 * LOG2E

VMEM_LIMIT_BYTES = 56 * 1024 * 1024
SUBLANES = 8
CONV_HALO = 32


def _params(*sem):
    return pltpu.CompilerParams(dimension_semantics=sem, vmem_limit_bytes=VMEM_LIMIT_BYTES)


def _rms(x):
    return x * lax.rsqrt(jnp.mean(x * x, axis=-1, keepdims=True) + EPS)


def _dot(a, b):
    return jnp.dot(a, b, preferred_element_type=F32)


def _dot_nt(a, b):
    return lax.dot_general(a, b, (((1,), (1,)), ((), ())), preferred_element_type=F32)


def _ffn_body(x_ref, g_ref, wg_ref, wu_ref, wd_ref, gn_ref, o_ref, *rest, emit_norm):
    if emit_norm:
        h_ref, n_scr = rest
    else:
        (n_scr,) = rest
    j = pl.program_id(1)

    @pl.when(j == 0)
    def _():
        x = x_ref[...]
        n_scr[...] = (_rms(x) * g_ref[...]).astype(BF16)
        o_ref[...] = x

    n = n_scr[...]
    gg = _dot(n, wg_ref[...])
    uu = _dot(n, wu_ref[...])
    act = (gg * jax.nn.sigmoid(gg) * uu * 0.5).astype(BF16)
    o_ref[...] += _dot(act, wd_ref[...])

    if emit_norm:

        @pl.when(j == pl.num_programs(1) - 1)
        def _():
            h_ref[...] = (_rms(o_ref[...]) * gn_ref[...]).astype(BF16)


def _ffn(x, g, w_gu, w_down, g_next, *, emit_norm, tm, tf):
    S, D = x.shape
    nf = D_FF // tf
    out_shape = [jax.ShapeDtypeStruct((S, D), F32)]
    out_specs = [pl.BlockSpec((tm, D), lambda i, j: (i, 0))]
    if emit_norm:
        out_shape.append(jax.ShapeDtypeStruct((S, D), BF16))
        out_specs.append(pl.BlockSpec((tm, D), lambda i, j: (i, 0)))
    return pl.pallas_call(
        functools.partial(_ffn_body, emit_norm=emit_norm),
        grid=(S // tm, nf),
        in_specs=[
            pl.BlockSpec((tm, D), lambda i, j: (i, 0)),
            pl.BlockSpec((1, D), lambda i, j: (0, 0)),
            pl.BlockSpec((D, tf), lambda i, j: (0, j)),
            pl.BlockSpec((D, tf), lambda i, j: (0, j + nf)),
            pl.BlockSpec((tf, D), lambda i, j: (j, 0)),
            pl.BlockSpec((1, D), lambda i, j: (0, 0)),
        ],
        out_specs=out_specs,
        out_shape=out_shape,
        scratch_shapes=[pltpu.VMEM((tm, D), BF16)],
        compiler_params=_params("parallel", "arbitrary"),
        name="ffn_norm" if emit_norm else "ffn",
    )(x, g, w_gu, w_gu, w_down, g_next)


def _gates_body(h_ref, w_ref, b_ref, o_ref):
    o_ref[...] = jax.nn.sigmoid(_dot(h_ref[...], w_ref[...]) + b_ref[...]).astype(BF16)


def _gates(h, w, b, *, tm, tn, col0):
    S, D = h.shape
    N = b.shape[1]
    j0 = col0 // tn
    return pl.pallas_call(
        _gates_body,
        grid=(S // tm, N // tn),
        in_specs=[
            pl.BlockSpec((tm, D), lambda i, j: (i, 0)),
            pl.BlockSpec((D, tn), lambda i, j: (0, j + j0)),
            pl.BlockSpec((1, tn), lambda i, j: (0, j)),
        ],
        out_specs=pl.BlockSpec((tm, tn), lambda i, j: (i, j)),
        out_shape=jax.ShapeDtypeStruct((S, N), BF16),
        compiler_params=_params("parallel", "arbitrary"),
        name="gates",
    )(h, w, b)


def _glu_body(h_ref, wa_ref, ws_ref, o_ref):
    h = h_ref[...]
    o_ref[...] = (_dot(h, wa_ref[...]) * jax.nn.sigmoid(_dot(h, ws_ref[...]))).astype(BF16)


def _glu(h, w, *, tm, tn):
    S, D = h.shape
    nn = CONV_CH // tn
    return pl.pallas_call(
        _glu_body,
        grid=(S // tm, nn),
        in_specs=[
            pl.BlockSpec((tm, D), lambda i, j: (i, 0)),
            pl.BlockSpec((D, tn), lambda i, j: (0, j)),
            pl.BlockSpec((D, tn), lambda i, j: (0, j + nn)),
        ],
        out_specs=pl.BlockSpec((tm, tn), lambda i, j: (i, j)),
        out_shape=jax.ShapeDtypeStruct((S, CONV_CH), BF16),
        compiler_params=_params("parallel", "arbitrary"),
        name="glu",
    )(h, w, w)


_SP_CQ = 0
_SP_CKV = _SP_CQ + Q_LORA
_SP_MQ = _SP_CKV + KV_LORA
_SP_P1 = _SP_MQ + MEM_HEADS * MEM_HEAD_DIM
_SP_P2 = _SP_P1 + 2 * QK_ROPE
_SP_COLS = _SP_P2 + 2 * QK_ROPE


def _small_proj_body(h_ref, w_ref, pos_ref, gqa_ref, gkva_ref, gmq_ref, freq_ref, kgc_ref, kgs_ref, qg_ref,
                     cq_ref, ckv_ref, mq_ref, kaux_ref, qtab_ref):
    r = _dot(h_ref[...], w_ref[...])
    cq_ref[...] = (_rms(r[:, _SP_CQ:_SP_CKV]) * gqa_ref[...]).astype(BF16)
    ckv_ref[...] = (_rms(r[:, _SP_CKV:_SP_MQ]) * gkva_ref[...]).astype(BF16)
    for hd in range(MEM_HEADS):
        lo = hd * MEM_HEAD_DIM
        m = r[:, _SP_MQ + lo:_SP_MQ + lo + MEM_HEAD_DIM]
        mq_ref[:, lo:lo + MEM_HEAD_DIM] = (_rms(m) * gmq_ref[...] * (MEM_HEAD_DIM ** -0.5)).astype(BF16)
    ang = pos_ref[...].astype(F32) * freq_ref[...]
    cos, sin = jnp.cos(ang), jnp.sin(ang)
    low = lax.broadcasted_iota(jnp.int32, ang.shape, 1) < QK_ROPE
    kaux_ref[...] = (r[:, _SP_P1:_SP_P2] * (kgc_ref[...] * jnp.where(low, cos, 1.0))
                     + r[:, _SP_P2:_SP_COLS] * (kgs_ref[...] * sin))
    qtab_ref[...] = qg_ref[...] * jnp.where(low, cos, sin)


def _small_proj(h, w, pos, gqa, gkva, gmq, freq, kgc, kgs, qg, *, tm, col0):
    S, D = h.shape
    row = lambda n: pl.BlockSpec((1, n), lambda i: (0, 0))
    return pl.pallas_call(
        _small_proj_body,
        grid=(S // tm,),
        in_specs=[
            pl.BlockSpec((tm, D), lambda i: (i, 0)),
            pl.BlockSpec((D, _SP_COLS), lambda i: (0, col0 // _SP_COLS)),
            pl.BlockSpec((tm, 1), lambda i: (i, 0)),
            row(Q_LORA), row(KV_LORA), row(MEM_HEAD_DIM),
            row(2 * QK_ROPE), row(2 * QK_ROPE), row(2 * QK_ROPE), row(2 * QK_ROPE),
        ],
        out_specs=[
            pl.BlockSpec((tm, Q_LORA), lambda i: (i, 0)),
            pl.BlockSpec((tm, KV_LORA), lambda i: (i, 0)),
            pl.BlockSpec((tm, MEM_HEADS * MEM_HEAD_DIM), lambda i: (i, 0)),
            pl.BlockSpec((tm, 2 * QK_ROPE), lambda i: (i, 0)),
            pl.BlockSpec((tm, 2 * QK_ROPE), lambda i: (i, 0)),
        ],
        out_shape=[
            jax.ShapeDtypeStruct((S, Q_LORA), BF16),
            jax.ShapeDtypeStruct((S, KV_LORA), BF16),
            jax.ShapeDtypeStruct((S, MEM_HEADS * MEM_HEAD_DIM), BF16),
            jax.ShapeDtypeStruct((S, 2 * QK_ROPE), F32),
            jax.ShapeDtypeStruct((S, 2 * QK_ROPE), F32),
        ],
        compiler_params=_params("parallel"),
        name="small_proj",
    )(h, w, pos, gqa, gkva, gmq, freq, kgc, kgs, qg)


def _qkv_body(cq_ref, ckv_ref, kaux_ref, qtab_ref, wq_ref, wkv_ref, gqn_ref, gkn_ref, q_ref, k_ref, v_ref):
    cq = cq_ref[...]
    ckv = ckv_ref[...]
    kaux = kaux_ref[...]
    qtab = qtab_ref[...]
    low = lax.broadcasted_iota(jnp.int32, kaux.shape, 1) < QK_ROPE
    kss = jnp.sum(jnp.where(low, 0.0, kaux * kaux), axis=-1, keepdims=True)
    k_rope = kaux[:, 0:QK_ROPE]
    nxt = (_dot(cq, wq_ref[0]), _dot(ckv, wkv_ref[0]))
    for hd in range(MLA_HEADS):
        qf, kv = nxt
        if hd + 1 < MLA_HEADS:
            nxt = (_dot(cq, wq_ref[hd + 1]), _dot(ckv, wkv_ref[hd + 1]))
        nope, rot = qf[:, 0:QK_NOPE], qf[:, QK_NOPE:]
        ss = jnp.sum(nope * nope + jnp.where(low, rot * rot, 0.0), axis=-1, keepdims=True)
        rq = lax.rsqrt(ss * (1.0 / QK_DIM) + EPS) * (QK_DIM ** -0.5 * LOG2E)
        q_ref[hd, :, 0:QK_NOPE] = (nope * rq * gqn_ref[...]).astype(BF16)
        mixed = rot * qtab
        mixed = mixed + pltpu.roll(mixed, QK_ROPE, axis=1)
        q_ref[hd, :, QK_NOPE:QK_DIM] = (mixed[:, 0:QK_ROPE] * rq).astype(BF16)

        kn = kv[:, 0:QK_NOPE]
        rk = lax.rsqrt((jnp.sum(kn * kn, axis=-1, keepdims=True) + kss) * (1.0 / QK_DIM) + EPS)
        k_ref[hd, :, 0:QK_NOPE] = (kn * rk * gkn_ref[...]).astype(BF16)
        k_ref[hd, :, QK_NOPE:QK_DIM] = (k_rope * rk).astype(BF16)
        v_ref[hd] = kv[:, QK_NOPE:QK_NOPE + V_HEAD].astype(BF16)


def _qkv_prep(cq, ckv, kaux, qtab, wq, wkv, gqn, gkn, *, tm):
    S = cq.shape[0]
    H = MLA_HEADS
    row = lambda n: pl.BlockSpec((1, n), lambda i: (0, 0))
    return pl.pallas_call(
        _qkv_body,
        grid=(S // tm,),
        in_specs=[
            pl.BlockSpec((tm, Q_LORA), lambda i: (i, 0)),
            pl.BlockSpec((tm, KV_LORA), lambda i: (i, 0)),
            pl.BlockSpec((tm, 2 * QK_ROPE), lambda i: (i, 0)),
            pl.BlockSpec((tm, 2 * QK_ROPE), lambda i: (i, 0)),
            pl.BlockSpec((H, Q_LORA, QK_DIM + QK_ROPE), lambda i: (0, 0, 0)),
            pl.BlockSpec((H, KV_LORA, QK_NOPE + V_HEAD), lambda i: (0, 0, 0)),
            row(QK_NOPE), row(QK_NOPE),
        ],
        out_specs=[
            pl.BlockSpec((H, tm, QK_DIM), lambda i: (0, i, 0)),
            pl.BlockSpec((H, tm, QK_DIM), lambda i: (0, i, 0)),
            pl.BlockSpec((H, tm, V_HEAD), lambda i: (0, i, 0)),
        ],
        out_shape=[
            jax.ShapeDtypeStruct((H, S, QK_DIM), BF16),
            jax.ShapeDtypeStruct((H, S, QK_DIM), BF16),
            jax.ShapeDtypeStruct((H, S, V_HEAD), BF16),
        ],
        compiler_params=_params("parallel"),
        name="qkv_prep",
    )(cq, ckv, kaux, qtab, wq, wkv, gqn, gkn)


def _attn_body(sc_ref, q_ref, k_ref, v_ref, o_ref, l_scr, acc_scr, m_scr, *, tq, tk, unroll):
    qi = pl.program_id(1)
    q = q_ref[0]
    n_diag = tq // tk
    n_full = qi * n_diag

    def kv_block(kb):
        start = pl.multiple_of(kb * tk, tk)
        return k_ref[0, pl.ds(start, tk), :], v_ref[0, pl.ds(start, tk), :]

    def causal(x, kb, fill):
        row = lax.broadcasted_iota(jnp.int32, x.shape, 0) + qi * tq
        col = lax.broadcasted_iota(jnp.int32, x.shape, 1) + kb * tk
        return jnp.where(col <= row, x, fill)

    @pl.when(sc_ref[1] > 0.0)
    def _bounded():
        bound = sc_ref[0]

        def block(kb, masked):
            k, v = kv_block(kb)
            p = jnp.exp2(_dot_nt(q, k) - bound)
            if masked:
                p = causal(p, kb, 0.0)
            lanes = p[:, 0:128]
            for c in range(128, tk, 128):
                lanes = lanes + p[:, c:c + 128]
            return lanes, _dot(p.astype(BF16), v)

        def add(kb, n, masked=False):
            lanes, pv = block(kb, masked)
            for u in range(1, n):
                lanes_u, pv_u = block(kb + u, masked)
                lanes, pv = lanes + lanes_u, pv + pv_u
            return lanes, pv

        l_scr[...] = jnp.zeros(l_scr.shape, F32)
        acc_scr[...] = jnp.zeros(acc_scr.shape, F32)

        def many(sb, carry):
            lanes, pv = add(sb * unroll, unroll)
            l_scr[...] += lanes
            acc_scr[...] += pv
            return carry

        def one(kb, carry):
            lanes, pv = add(kb, 1)
            l_scr[...] += lanes
            acc_scr[...] += pv
            return carry

        n_many = n_full // unroll
        lax.fori_loop(0, n_many, many, 0)
        lax.fori_loop(n_many * unroll, n_full, one, 0)
        lanes, pv = add(n_full, n_diag, masked=True)
        l = jnp.sum(l_scr[...] + lanes, axis=-1, keepdims=True)
        o_ref[...] = ((acc_scr[...] + pv) / l).astype(BF16)

    @pl.when(sc_ref[1] <= 0.0)
    def _online():
        m_scr[...] = jnp.full(m_scr.shape, -jnp.inf, F32)
        l_scr[...] = jnp.zeros(l_scr.shape, F32)
        acc_scr[...] = jnp.zeros(acc_scr.shape, F32)

        def step(kb, masked):
            k, v = kv_block(kb)
            s = _dot_nt(q, k)
            if masked:
                s = causal(s, kb, -jnp.inf)
            m_old = m_scr[...]
            m_new = jnp.maximum(m_old, jnp.max(s, axis=-1, keepdims=True))
            p = jnp.exp2(s - m_new)
            alpha = jnp.exp2(m_old - m_new)
            l_scr[:, 0:1] = alpha * l_scr[:, 0:1] + jnp.sum(p, axis=-1, keepdims=True)
            acc_scr[...] = alpha * acc_scr[...] + _dot(p.astype(BF16), v)
            m_scr[...] = m_new

        def body(kb, carry):
            step(kb, False)
            return carry

        lax.fori_loop(0, n_full, body, 0)
        for d in range(n_diag):
            step(n_full + d, True)
        o_ref[...] = (acc_scr[...] / l_scr[:, 0:1]).astype(BF16)


def _attention(sc, q, k, v, *, tq, tk, unroll):
    H, S, _ = q.shape
    return pl.pallas_call(
        functools.partial(_attn_body, tq=tq, tk=tk, unroll=unroll),
        grid_spec=pltpu.PrefetchScalarGridSpec(
            num_scalar_prefetch=1,
            grid=(H, S // tq),
            in_specs=[
                pl.BlockSpec((1, tq, QK_DIM), lambda h, i, sc: (h, i, 0)),
                pl.BlockSpec((1, S, QK_DIM), lambda h, i, sc: (h, 0, 0)),
                pl.BlockSpec((1, S, V_HEAD), lambda h, i, sc: (h, 0, 0)),
            ],
            out_specs=pl.BlockSpec((tq, V_HEAD), lambda h, i, sc: (i, h)),
            scratch_shapes=[pltpu.VMEM((tq, 128), F32), pltpu.VMEM((tq, V_HEAD), F32), pltpu.VMEM((tq, 1), F32)],
        ),
        out_shape=jax.ShapeDtypeStruct((S, H * V_HEAD), BF16),
        compiler_params=_params("parallel", "arbitrary"),
        name="attention",
    )(sc, q, k, v)


def _conv_body(cur_ref, halo_ref, w_ref, b_ref, lg_ref, lb_ref, o_ref, ext_scr, sh_scr, conv_scr, *, tm, rb):
    halo = halo_ref[...].astype(F32)
    ext_scr[0:CONV_HALO, :] = jnp.where(pl.program_id(0) == 0, 0.0, halo)
    ext_scr[CONV_HALO:CONV_HALO + tm, :] = cur_ref[...].astype(F32)
    n_sh = sh_scr.shape[1]
    for r in range(1, SUBLANES):
        sh_scr[r - 1] = ext_scr[r:r + n_sh, :]
    base = CONV_HALO - (CONV_WIDTH - 1)
    for c0 in range(0, CONV_CH, 128):
        for r0 in range(0, tm, rb):
            acc = jnp.broadcast_to(b_ref[:, c0:c0 + 128], (rb, 128))
            for j in range(CONV_WIDTH):
                r = (base + j) % SUBLANES
                lo = r0 + base + j - r
                if r == 0:
                    rows = ext_scr[lo:lo + rb, c0:c0 + 128]
                else:
                    rows = sh_scr[r - 1, lo:lo + rb, c0:c0 + 128]
                acc = acc + rows * w_ref[j:j + 1, c0:c0 + 128]
            conv_scr[r0:r0 + rb, c0:c0 + 128] = acc
    a = conv_scr[...]
    xc = a - jnp.mean(a, axis=-1, keepdims=True)
    y = xc * lax.rsqrt(jnp.mean(xc * xc, axis=-1, keepdims=True) + EPS) * lg_ref[...] + lb_ref[...]
    o_ref[...] = (y * jax.nn.sigmoid(y)).astype(BF16)


def _conv(a, w, b, lg, lb, *, tm, rb):
    S, C = a.shape
    hb = tm // CONV_HALO
    row = lambda: pl.BlockSpec((1, C), lambda i: (0, 0))
    return pl.pallas_call(
        functools.partial(_conv_body, tm=tm, rb=rb),
        grid=(S // tm,),
        in_specs=[
            pl.BlockSpec((tm, C), lambda i: (i, 0)),
            pl.BlockSpec((CONV_HALO, C), lambda i: (jnp.maximum(i * hb - 1, 0), 0)),
            pl.BlockSpec((CONV_WIDTH, C), lambda i: (0, 0)),
            row(), row(), row(),
        ],
        out_specs=pl.BlockSpec((tm, C), lambda i: (i, 0)),
        out_shape=jax.ShapeDtypeStruct((S, C), BF16),
        scratch_shapes=[
            pltpu.VMEM((CONV_HALO + tm, C), F32),
            pltpu.VMEM((SUBLANES - 1, CONV_HALO + tm - SUBLANES, C), F32),
            pltpu.VMEM((tm, C), F32),
        ],
        compiler_params=_params("parallel"),
        name="conv",
    )(a, a, w, b, lg, lb)


def _mem_kv_body(mem_ref, g_ref, w_ref, gk_ref, mk_ref, mv_ref):
    n = (_rms(mem_ref[...]) * g_ref[...]).astype(BF16)
    kv = _dot(n, w_ref[...])
    width = MEM_HEADS * MEM_HEAD_DIM
    for hd in range(MEM_HEADS):
        lo = hd * MEM_HEAD_DIM
        mk_ref[:, lo:lo + MEM_HEAD_DIM] = (_rms(kv[:, lo:lo + MEM_HEAD_DIM]) * gk_ref[...]).astype(BF16)
    mv_ref[...] = kv[:, width:2 * width].astype(BF16)


def _mem_kv(mem, g, w, gk):
    M, D = mem.shape
    width = MEM_HEADS * MEM_HEAD_DIM
    full = lambda shape: pl.BlockSpec(shape, lambda i: (0,) * len(shape))
    return pl.pallas_call(
        _mem_kv_body,
        grid=(1,),
        in_specs=[full((M, D)), full((1, D)), full((D, 2 * width)), full((1, MEM_HEAD_DIM))],
        out_specs=[full((M, width)), full((M, width))],
        out_shape=[jax.ShapeDtypeStruct((M, width), BF16), jax.ShapeDtypeStruct((M, width), BF16)],
        compiler_params=_params("arbitrary"),
        name="mem_kv",
    )(mem, g, w, gk)


def _mem_attn_body(mq_ref, mk_ref, mv_ref, o_ref):
    for hd in range(MEM_HEADS):
        lo = hd * MEM_HEAD_DIM
        s = _dot_nt(mq_ref[:, lo:lo + MEM_HEAD_DIM], mk_ref[:, lo:lo + MEM_HEAD_DIM])
        p = jnp.exp(s - jnp.max(s, axis=-1, keepdims=True))
        p = p / jnp.sum(p, axis=-1, keepdims=True)
        o_ref[:, lo:lo + MEM_HEAD_DIM] = _dot(p.astype(BF16), mv_ref[:, lo:lo + MEM_HEAD_DIM]).astype(BF16)


def _mem_attn(mq, mk, mv, *, tm):
    S, W = mq.shape
    M = mk.shape[0]
    return pl.pallas_call(
        _mem_attn_body,
        grid=(S // tm,),
        in_specs=[
            pl.BlockSpec((tm, W), lambda i: (i, 0)),
            pl.BlockSpec((M, W), lambda i: (0, 0)),
            pl.BlockSpec((M, W), lambda i: (0, 0)),
        ],
        out_specs=pl.BlockSpec((tm, W), lambda i: (i, 0)),
        out_shape=jax.ShapeDtypeStruct((S, W), BF16),
        compiler_params=_params("parallel"),
        name="mem_attn",
    )(mq, mk, mv)


def _resident(shape):
    return pl.BlockSpec(shape, lambda i: (0,) * len(shape), pipeline_mode=pl.Buffered(1))


def _merge_body(c_ref, o_ref, om_ref, g0_ref, g1_ref, g2_ref, wc_ref, wo_ref, wm_ref, out_ref, *, tn):
    c, o, om = c_ref[...], o_ref[...], om_ref[...]
    for n0 in range(0, out_ref.shape[1], tn):
        cols = slice(n0, n0 + tn)
        out_ref[:, cols] = (g0_ref[:, cols].astype(F32) * _dot(c, wc_ref[:, cols])
                            + g1_ref[:, cols].astype(F32) * _dot(o, wo_ref[:, cols])
                            + g2_ref[:, cols].astype(F32) * _dot(om, wm_ref[:, cols])).astype(BF16)


def _merge(c, o, om, gates, wc, wo, wm, *, tm, tn):
    S = c.shape[0]
    D = wc.shape[1]
    rows = lambda a: pl.BlockSpec((tm, a.shape[1]), lambda i: (i, 0))
    return pl.pallas_call(
        functools.partial(_merge_body, tn=tn),
        grid=(S // tm,),
        in_specs=[
            rows(c), rows(o), rows(om),
            pl.BlockSpec((tm, D), lambda i: (i, 0)),
            pl.BlockSpec((tm, D), lambda i: (i, 1)),
            pl.BlockSpec((tm, D), lambda i: (i, 2)),
            _resident(wc.shape), _resident(wo.shape), _resident(wm.shape),
        ],
        out_specs=pl.BlockSpec((tm, D), lambda i: (i, 0)),
        out_shape=jax.ShapeDtypeStruct((S, D), BF16),
        compiler_params=_params("parallel"),
        name="merge",
    )(c, o, om, gates, gates, gates, wc, wo, wm)


def _out_proj_body(x_ref, m_ref, w_ref, out_ref, *, tn):
    m = m_ref[...]
    for n0 in range(0, out_ref.shape[1], tn):
        cols = slice(n0, n0 + tn)
        out_ref[:, cols] = x_ref[:, cols] + _dot(m, w_ref[:, cols])


def _out_proj(x, m, w, *, tm, tn):
    S, D = x.shape
    return pl.pallas_call(
        functools.partial(_out_proj_body, tn=tn),
        grid=(S // tm,),
        in_specs=[
            pl.BlockSpec((tm, D), lambda i: (i, 0)),
            pl.BlockSpec((tm, D), lambda i: (i, 0)),
            _resident(w.shape),
        ],
        out_specs=pl.BlockSpec((tm, D), lambda i: (i, 0)),
        out_shape=jax.ShapeDtypeStruct((S, D), F32),
        compiler_params=_params("parallel"),
        name="out_proj",
    )(x, m, w)


def _swap_halves(v):
    return jnp.concatenate([v[..., HALF_ROPE:], v[..., :HALF_ROPE]], axis=-1)


def _tile(S, want):
    return min(S, want)


def _layer(x, mem, pos, p, l):
    S = x.shape[0]
    row = lambda v: v.reshape(1, -1).astype(F32)
    bf = lambda w: w.astype(BF16)

    inv_freq = ROPE_THETA ** (-jnp.arange(0, QK_ROPE, 2, dtype=F32) / QK_ROPE)
    freq64 = jnp.concatenate([inv_freq, inv_freq])
    sign64 = jnp.concatenate([-jnp.ones((HALF_ROPE,), F32), jnp.ones((HALF_ROPE,), F32)])
    zeros64 = jnp.zeros((QK_ROPE,), F32)

    w_in = p["w_in"][l]
    c0 = 2 * CONV_CH
    c1 = c0 + Q_LORA
    c2 = c1 + KV_LORA
    c3 = c2 + QK_ROPE
    c4 = c3 + MEM_HEADS * MEM_HEAD_DIM
    w_kpe = w_in[:, c2:c3]
    gate_col0 = c0
    small_col0 = c0 + N_BRANCH * D_MODEL
    assert gate_col0 % 1024 == 0 and small_col0 % _SP_COLS == 0
    w_proj = bf(jnp.concatenate(
        [w_in[:, :c0], w_in[:, c4:], w_in[:, c0:c1], w_in[:, c1:c2], w_in[:, c3:c4],
         w_kpe, w_kpe, _swap_halves(w_kpe), jnp.zeros_like(w_kpe)], axis=1))

    w_uq = p["w_uq"][l].reshape(Q_LORA, MLA_HEADS, QK_DIM).transpose(1, 0, 2)
    wq = bf(jnp.concatenate([w_uq, _swap_halves(w_uq[..., QK_NOPE:])], axis=-1))
    w_kv = bf(p["w_ukv"][l].reshape(KV_LORA, MLA_HEADS, QK_NOPE + V_HEAD).transpose(1, 0, 2))

    g_q = p["g_qnorm"][l]
    g_k = p["g_knorm"][l]
    gq_rope, gk_rope = g_q[QK_NOPE:], g_k[QK_NOPE:]
    k_gc = jnp.concatenate([gk_rope, jnp.ones((QK_ROPE,), F32)])
    k_gs = jnp.concatenate([sign64 * _swap_halves(gk_rope), zeros64])
    q_g = jnp.concatenate([gq_rope, sign64 * _swap_halves(gq_rope)])
    freq128 = jnp.concatenate([freq64, freq64])

    x1, h = _ffn(x, row(p["g_ffn1"][l]), bf(p["w_ffn1_gu"][l]), bf(p["w_ffn1_down"][l]), row(p["g_mix"][l]),
                 emit_norm=True, tm=_tile(S, 512), tf=512)

    gates = _gates(h, w_proj, row(p["b_gate"][l]), tm=_tile(S, 1024), tn=1024, col0=gate_col0)
    a = _glu(h, w_proj, tm=_tile(S, 1024), tn=512)
    cq, ckv, mq, kaux, qtab = _small_proj(
        h, w_proj, pos, row(p["g_q_a"][l]), row(p["g_kv_a"][l]), row(p["g_mqnorm"][l]),
        row(freq128), row(k_gc), row(k_gs), row(q_g), tm=_tile(S, 512), col0=small_col0)

    conv = _conv(a, p["conv_w"][l].astype(F32), row(p["conv_b"][l]), row(p["conv_ln_g"][l]),
                 row(p["conv_ln_b"][l]), tm=_tile(S, 256), rb=64)

    q, k, v = _qkv_prep(cq, ckv, kaux, qtab, wq, w_kv, row(g_q[:QK_NOPE]), row(g_k[:QK_NOPE]), tm=_tile(S, 512))
    bound = (QK_DIM ** 0.5 * LOG2E * 1.01) * jnp.max(jnp.abs(g_q)) * jnp.max(jnp.abs(g_k))
    sc = jnp.stack([bound, (bound <= MAX_STABLE_BOUND_LOG2).astype(F32)])
    o = _attention(sc, q, k, v, tq=_tile(S, 1024), tk=512, unroll=4)

    mk, mv = _mem_kv(mem, row(p["g_mem"][l]), bf(p["w_mem_kv"][l]), row(p["g_mknorm"][l]))
    om = _mem_attn(mq, mk, mv, tm=_tile(S, 512))

    merged = _merge(conv, o, om, gates, bf(p["w_conv_out"][l]), bf(p["w_mla_out"][l]), bf(p["w_mem_out"][l]),
                    tm=_tile(S, 512), tn=512)
    x2 = _out_proj(x1, merged, bf(p["w_out"][l]), tm=_tile(S, 512), tn=512)

    (x3,) = _ffn(x2, row(p["g_ffn2"][l]), bf(p["w_ffn2_gu"][l]), bf(p["w_ffn2_down"][l]), row(p["g_ffn2"][l]),
                 emit_norm=False, tm=_tile(S, 512), tf=512)
    return x3


def kernel(x, mem, positions, g_ffn1, w_ffn1_gu, w_ffn1_down, g_mix, w_in, b_gate, conv_w, conv_b, conv_ln_g, conv_ln_b, w_conv_out, g_q_a, w_uq, g_kv_a, w_ukv, g_qnorm, g_knorm, w_mla_out, g_mem, w_mem_kv, g_mqnorm, g_mknorm, w_mem_out, w_out, g_ffn2, w_ffn2_gu, w_ffn2_down):
    p = dict(g_ffn1=g_ffn1, w_ffn1_gu=w_ffn1_gu, w_ffn1_down=w_ffn1_down, g_mix=g_mix, w_in=w_in, b_gate=b_gate,
             conv_w=conv_w, conv_b=conv_b, conv_ln_g=conv_ln_g, conv_ln_b=conv_ln_b, w_conv_out=w_conv_out,
             g_q_a=g_q_a, w_uq=w_uq, g_kv_a=g_kv_a, w_ukv=w_ukv, g_qnorm=g_qnorm, g_knorm=g_knorm,
             w_mla_out=w_mla_out, g_mem=g_mem, w_mem_kv=w_mem_kv, g_mqnorm=g_mqnorm, g_mknorm=g_mknorm,
             w_mem_out=w_mem_out, w_out=w_out, g_ffn2=g_ffn2, w_ffn2_gu=w_ffn2_gu, w_ffn2_down=w_ffn2_down)
    B, S, D = x.shape
    depth = g_ffn1.shape[0]
    outs = []
    for b in range(B):
        xb = x.reshape(S, D) if B == 1 else x[b]
        pos = positions[b].reshape(S, 1)
        for l in range(depth):
            xb = _layer(xb, mem[b], pos, p, l)
        outs.append(xb.reshape(1, S, D))
    return outs[0] if B == 1 else jnp.concatenate(outs, axis=0)
```

```python
import functools

import jax
import jax.numpy as jnp
from jax import lax
from jax.experimental import pallas as pl
from jax.experimental.pallas import tpu as pltpu

F32 = jnp.float32
BF16 = jnp.bfloat16

D_MODEL = 2048
MEM_LEN = 256
D_FF = 5632
CONV_CH = 1536
CONV_WIDTH = 31
MLA_HEADS = 12
Q_LORA = 512
KV_LORA = 512
QK_NOPE = 128
QK_ROPE = 64
QK_DIM = QK_NOPE + QK_ROPE
V_HEAD = 128
MEM_HEADS = 4
MEM_HEAD_DIM = 256
N_BRANCH = 3
ROPE_THETA = 10000.0
EPS = 1e-6
HALF_ROPE = QK_ROPE // 2
LOG2E = 1.4426950408889634
MAX_STABLE_BOUND_LOG2 = 30.0 * LOG2E

VMEM_LIMIT_BYTES = 56 * 1024 * 1024
SUBLANES = 8
LANES = 128
CONV_HALO = 32


def _params(*sem):
    return pltpu.CompilerParams(dimension_semantics=sem, vmem_limit_bytes=VMEM_LIMIT_BYTES)


def _rms(x):
    return x * lax.rsqrt(jnp.mean(x * x, axis=-1, keepdims=True) + EPS)


def _dot(a, b):
    return jnp.dot(a, b, preferred_element_type=F32)


def _dot_nt(a, b):
    return lax.dot_general(a, b, (((1,), (1,)), ((), ())), preferred_element_type=F32)


def _ffn_body(x_ref, g_ref, wg_ref, wu_ref, wd_ref, gn_ref, o_ref, *rest, emit_norm):
    if emit_norm:
        h_ref, n_scr = rest
    else:
        (n_scr,) = rest
    j = pl.program_id(1)

    @pl.when(j == 0)
    def _():
        x = x_ref[...]
        n_scr[...] = (_rms(x) * g_ref[...]).astype(BF16)
        o_ref[...] = x

    n = n_scr[...]
    gg = _dot(n, wg_ref[...])
    uu = _dot(n, wu_ref[...])
    act = (gg * jax.nn.sigmoid(gg) * uu * 0.5).astype(BF16)
    o_ref[...] += _dot(act, wd_ref[...])

    if emit_norm:

        @pl.when(j == pl.num_programs(1) - 1)
        def _():
            h_ref[...] = (_rms(o_ref[...]) * gn_ref[...]).astype(BF16)


def _ffn(x, g, w_gu, w_down, g_next, *, emit_norm, tm, tf):
    S, D = x.shape
    nf = D_FF // tf
    out_shape = [jax.ShapeDtypeStruct((S, D), F32)]
    out_specs = [pl.BlockSpec((tm, D), lambda i, j: (i, 0))]
    if emit_norm:
        out_shape.append(jax.ShapeDtypeStruct((S, D), BF16))
        out_specs.append(pl.BlockSpec((tm, D), lambda i, j: (i, 0)))
    return pl.pallas_call(
        functools.partial(_ffn_body, emit_norm=emit_norm),
        grid=(S // tm, nf),
        in_specs=[
            pl.BlockSpec((tm, D), lambda i, j: (i, 0)),
            pl.BlockSpec((1, D), lambda i, j: (0, 0)),
            pl.BlockSpec((D, tf), lambda i, j: (0, j)),
            pl.BlockSpec((D, tf), lambda i, j: (0, j + nf)),
            pl.BlockSpec((tf, D), lambda i, j: (j, 0)),
            pl.BlockSpec((1, D), lambda i, j: (0, 0)),
        ],
        out_specs=out_specs,
        out_shape=out_shape,
        scratch_shapes=[pltpu.VMEM((tm, D), BF16)],
        compiler_params=_params("parallel", "arbitrary"),
        name="ffn_norm" if emit_norm else "ffn",
    )(x, g, w_gu, w_gu, w_down, g_next)


def _gates_body(h_ref, w_ref, wnext_ref, b_ref, o_ref, w_scr, *, shift):
    @pl.when(pl.program_id(1) == 0)
    def _():
        tn = w_scr.shape[1]
        wide = jnp.concatenate([w_ref[...], wnext_ref[...]], axis=1)
        w_scr[...] = wide[:, shift:shift + tn].astype(BF16)

    o_ref[...] = jax.nn.sigmoid(_dot(h_ref[...], w_scr[...]) + b_ref[...]).astype(BF16)


def _gates(h, w, b, *, tm, tn, col0):
    S, D = h.shape
    N = b.shape[1]
    shift = col0 % tn
    base = col0 - shift
    assert base % tn == 0 and shift < LANES
    j0 = base // tn
    per = tn // LANES
    return pl.pallas_call(
        functools.partial(_gates_body, shift=shift),
        grid=(N // tn, S // tm),
        in_specs=[
            pl.BlockSpec((tm, D), lambda j, i: (i, 0)),
            pl.BlockSpec((D, tn), lambda j, i: (0, j + j0)),
            pl.BlockSpec((D, LANES), lambda j, i: (0, (j + j0 + 1) * per)),
            pl.BlockSpec((1, tn), lambda j, i: (0, j)),
        ],
        out_specs=pl.BlockSpec((tm, tn), lambda j, i: (i, j)),
        out_shape=jax.ShapeDtypeStruct((S, N), BF16),
        scratch_shapes=[pltpu.VMEM((D, tn), BF16)],
        compiler_params=_params("parallel", "arbitrary"),
        name="gates",
    )(h, w, w, b)


def _glu_body(h_ref, wa_ref, ws_ref, o_ref, wa_scr, ws_scr):
    @pl.when(pl.program_id(1) == 0)
    def _():
        wa_scr[...] = wa_ref[...].astype(BF16)
        ws_scr[...] = ws_ref[...].astype(BF16)

    h = h_ref[...]
    o_ref[...] = (_dot(h, wa_scr[...]) * jax.nn.sigmoid(_dot(h, ws_scr[...]))).astype(BF16)


def _glu(h, w, *, tm, tn):
    S, D = h.shape
    nn = CONV_CH // tn
    return pl.pallas_call(
        _glu_body,
        grid=(nn, S // tm),
        in_specs=[
            pl.BlockSpec((tm, D), lambda j, i: (i, 0)),
            pl.BlockSpec((D, tn), lambda j, i: (0, j)),
            pl.BlockSpec((D, tn), lambda j, i: (0, j + nn)),
        ],
        out_specs=pl.BlockSpec((tm, tn), lambda j, i: (i, j)),
        out_shape=jax.ShapeDtypeStruct((S, CONV_CH), BF16),
        scratch_shapes=[pltpu.VMEM((D, tn), BF16), pltpu.VMEM((D, tn), BF16)],
        compiler_params=_params("parallel", "arbitrary"),
        name="glu",
    )(h, w, w)


_SP_CQ = 0
_SP_CKV = _SP_CQ + Q_LORA
_SP_MQ = _SP_CKV + KV_LORA
_SP_P1 = _SP_MQ + MEM_HEADS * MEM_HEAD_DIM
_SP_P2 = _SP_P1 + 2 * QK_ROPE
_SP_COLS = _SP_P2 + 2 * QK_ROPE


def _small_proj_body(h_ref, w_ref, pos_ref, gqa_ref, gkva_ref, gmq_ref, freq_ref, kgc_ref, kgs_ref, qg_ref,
                     cq_ref, ckv_ref, mq_ref, kaux_ref, qtab_ref):
    r = _dot(h_ref[...], w_ref[...])
    cq_ref[...] = (_rms(r[:, _SP_CQ:_SP_CKV]) * gqa_ref[...]).astype(BF16)
    ckv_ref[...] = (_rms(r[:, _SP_CKV:_SP_MQ]) * gkva_ref[...]).astype(BF16)
    for hd in range(MEM_HEADS):
        lo = hd * MEM_HEAD_DIM
        m = r[:, _SP_MQ + lo:_SP_MQ + lo + MEM_HEAD_DIM]
        mq_ref[:, lo:lo + MEM_HEAD_DIM] = (_rms(m) * gmq_ref[...] * (MEM_HEAD_DIM ** -0.5)).astype(BF16)
    ang = pos_ref[...].astype(F32) * freq_ref[...]
    cos, sin = jnp.cos(ang), jnp.sin(ang)
    low = lax.broadcasted_iota(jnp.int32, ang.shape, 1) < QK_ROPE
    kaux_ref[...] = (r[:, _SP_P1:_SP_P2] * (kgc_ref[...] * jnp.where(low, cos, 1.0))
                     + r[:, _SP_P2:_SP_COLS] * (kgs_ref[...] * sin))
    qtab_ref[...] = qg_ref[...] * jnp.where(low, cos, sin)


def _small_proj(h, w, pos, gqa, gkva, gmq, freq, kgc, kgs, qg, *, tm, col0):
    S, D = h.shape
    row = lambda n: pl.BlockSpec((1, n), lambda i: (0, 0))
    return pl.pallas_call(
        _small_proj_body,
        grid=(S // tm,),
        in_specs=[
            pl.BlockSpec((tm, D), lambda i: (i, 0)),
            pl.BlockSpec((D, _SP_COLS), lambda i: (0, col0 // _SP_COLS)),
            pl.BlockSpec((tm, 1), lambda i: (i, 0)),
            row(Q_LORA), row(KV_LORA), row(MEM_HEAD_DIM),
            row(2 * QK_ROPE), row(2 * QK_ROPE), row(2 * QK_ROPE), row(2 * QK_ROPE),
        ],
        out_specs=[
            pl.BlockSpec((tm, Q_LORA), lambda i: (i, 0)),
            pl.BlockSpec((tm, KV_LORA), lambda i: (i, 0)),
            pl.BlockSpec((tm, MEM_HEADS * MEM_HEAD_DIM), lambda i: (i, 0)),
            pl.BlockSpec((tm, 2 * QK_ROPE), lambda i: (i, 0)),
            pl.BlockSpec((tm, 2 * QK_ROPE), lambda i: (i, 0)),
        ],
        out_shape=[
            jax.ShapeDtypeStruct((S, Q_LORA), BF16),
            jax.ShapeDtypeStruct((S, KV_LORA), BF16),
            jax.ShapeDtypeStruct((S, MEM_HEADS * MEM_HEAD_DIM), BF16),
            jax.ShapeDtypeStruct((S, 2 * QK_ROPE), F32),
            jax.ShapeDtypeStruct((S, 2 * QK_ROPE), F32),
        ],
        compiler_params=_params("parallel"),
        name="small_proj",
    )(h, w, pos, gqa, gkva, gmq, freq, kgc, kgs, qg)


def _qkv_body(cq_ref, ckv_ref, kaux_ref, qtab_ref, wq_ref, wkv_ref, gqn_ref, gkn_ref, q_ref, k_ref, v_ref):
    cq = cq_ref[...]
    ckv = ckv_ref[...]
    kaux = kaux_ref[...]
    qtab = qtab_ref[...]
    low = lax.broadcasted_iota(jnp.int32, kaux.shape, 1) < QK_ROPE
    kss = jnp.sum(jnp.where(low, 0.0, kaux * kaux), axis=-1, keepdims=True)
    k_rope = kaux[:, 0:QK_ROPE]
    nxt = (_dot(cq, wq_ref[0]), _dot(ckv, wkv_ref[0]))
    for hd in range(MLA_HEADS):
        qf, kv = nxt
        if hd + 1 < MLA_HEADS:
            nxt = (_dot(cq, wq_ref[hd + 1]), _dot(ckv, wkv_ref[hd + 1]))
        nope, rot = qf[:, 0:QK_NOPE], qf[:, QK_NOPE:]
        ss = jnp.sum(nope * nope + jnp.where(low, rot * rot, 0.0), axis=-1, keepdims=True)
        rq = lax.rsqrt(ss * (1.0 / QK_DIM) + EPS) * (QK_DIM ** -0.5 * LOG2E)
        q_ref[hd, :, 0:QK_NOPE] = (nope * rq * gqn_ref[...]).astype(BF16)
        mixed = rot * qtab
        mixed = mixed + pltpu.roll(mixed, QK_ROPE, axis=1)
        q_ref[hd, :, QK_NOPE:QK_DIM] = (mixed[:, 0:QK_ROPE] * rq).astype(BF16)

        kn = kv[:, 0:QK_NOPE]
        rk = lax.rsqrt((jnp.sum(kn * kn, axis=-1, keepdims=True) + kss) * (1.0 / QK_DIM) + EPS)
        k_ref[hd, :, 0:QK_NOPE] = (kn * rk * gkn_ref[...]).astype(BF16)
        k_ref[hd, :, QK_NOPE:QK_DIM] = (k_rope * rk).astype(BF16)
        v_ref[hd] = kv[:, QK_NOPE:QK_NOPE + V_HEAD].astype(BF16)


def _qkv_prep(cq, ckv, kaux, qtab, wq, wkv, gqn, gkn, *, tm):
    S = cq.shape[0]
    H = MLA_HEADS
    row = lambda n: pl.BlockSpec((1, n), lambda i: (0, 0))
    return pl.pallas_call(
        _qkv_body,
        grid=(S // tm,),
        in_specs=[
            pl.BlockSpec((tm, Q_LORA), lambda i: (i, 0)),
            pl.BlockSpec((tm, KV_LORA), lambda i: (i, 0)),
            pl.BlockSpec((tm, 2 * QK_ROPE), lambda i: (i, 0)),
            pl.BlockSpec((tm, 2 * QK_ROPE), lambda i: (i, 0)),
            pl.BlockSpec((H, Q_LORA, QK_DIM + QK_ROPE), lambda i: (0, 0, 0)),
            pl.BlockSpec((H, KV_LORA, QK_NOPE + V_HEAD), lambda i: (0, 0, 0)),
            row(QK_NOPE), row(QK_NOPE),
        ],
        out_specs=[
            pl.BlockSpec((H, tm, QK_DIM), lambda i: (0, i, 0)),
            pl.BlockSpec((H, tm, QK_DIM), lambda i: (0, i, 0)),
            pl.BlockSpec((H, tm, V_HEAD), lambda i: (0, i, 0)),
        ],
        out_shape=[
            jax.ShapeDtypeStruct((H, S, QK_DIM), BF16),
            jax.ShapeDtypeStruct((H, S, QK_DIM), BF16),
            jax.ShapeDtypeStruct((H, S, V_HEAD), BF16),
        ],
        compiler_params=_params("parallel"),
        name="qkv_prep",
    )(cq, ckv, kaux, qtab, wq, wkv, gqn, gkn)


def _attn_body(sc_ref, q_ref, k_ref, v_ref, o_ref, l_scr, acc_scr, m_scr, *, tq, tk, unroll):
    qi = pl.program_id(1)
    q = q_ref[0]
    n_diag = tq // tk
    n_full = qi * n_diag

    def kv_block(kb):
        start = pl.multiple_of(kb * tk, tk)
        return k_ref[0, pl.ds(start, tk), :], v_ref[0, pl.ds(start, tk), :]

    def causal(x, kb, fill, row0=0):
        row = lax.broadcasted_iota(jnp.int32, x.shape, 0) + (qi * tq + row0)
        col = lax.broadcasted_iota(jnp.int32, x.shape, 1) + kb * tk
        return jnp.where(col <= row, x, fill)

    @pl.when(sc_ref[1] > 0.0)
    def _bounded():
        bound = sc_ref[0]

        def block(kb, masked, row0=0):
            k, v = kv_block(kb)
            p = jnp.exp2(_dot_nt(q[row0:], k) - bound)
            if masked:
                p = causal(p, kb, 0.0, row0)
            lanes = p[:, 0:128]
            for c in range(128, tk, 128):
                lanes = lanes + p[:, c:c + 128]
            return lanes, _dot(p.astype(BF16), v)

        def add(kb, n, masked=False):
            lanes, pv = block(kb, masked)
            for u in range(1, n):
                lanes_u, pv_u = block(kb + u, masked)
                lanes, pv = lanes + lanes_u, pv + pv_u
            return lanes, pv

        l_scr[...] = jnp.zeros(l_scr.shape, F32)
        acc_scr[...] = jnp.zeros(acc_scr.shape, F32)

        def many(sb, carry):
            lanes, pv = add(sb * unroll, unroll)
            l_scr[...] += lanes
            acc_scr[...] += pv
            return carry

        def one(kb, carry):
            lanes, pv = add(kb, 1)
            l_scr[...] += lanes
            acc_scr[...] += pv
            return carry

        n_many = n_full // unroll
        lax.fori_loop(0, n_many, many, 0)
        lax.fori_loop(n_many * unroll, n_full, one, 0)
        for d in range(n_diag):
            lanes, pv = block(n_full + d, True, row0=d * tk)
            l_scr[d * tk:, :] += lanes
            acc_scr[d * tk:, :] += pv
        l = jnp.sum(l_scr[...], axis=-1, keepdims=True)
        o_ref[...] = (acc_scr[...] / l).astype(BF16)

    @pl.when(sc_ref[1] <= 0.0)
    def _online():
        m_scr[...] = jnp.full(m_scr.shape, -jnp.inf, F32)
        l_scr[...] = jnp.zeros(l_scr.shape, F32)
        acc_scr[...] = jnp.zeros(acc_scr.shape, F32)

        def step(kb, masked):
            k, v = kv_block(kb)
            s = _dot_nt(q, k)
            if masked:
                s = causal(s, kb, -jnp.inf)
            m_old = m_scr[...]
            m_new = jnp.maximum(m_old, jnp.max(s, axis=-1, keepdims=True))
            p = jnp.exp2(s - m_new)
            alpha = jnp.exp2(m_old - m_new)
            l_scr[:, 0:1] = alpha * l_scr[:, 0:1] + jnp.sum(p, axis=-1, keepdims=True)
            acc_scr[...] = alpha * acc_scr[...] + _dot(p.astype(BF16), v)
            m_scr[...] = m_new

        def body(kb, carry):
            step(kb, False)
            return carry

        lax.fori_loop(0, n_full, body, 0)
        for d in range(n_diag):
            step(n_full + d, True)
        o_ref[...] = (acc_scr[...] / l_scr[:, 0:1]).astype(BF16)


def _attention(sc, q, k, v, *, tq, tk, unroll):
    H, S, _ = q.shape
    return pl.pallas_call(
        functools.partial(_attn_body, tq=tq, tk=tk, unroll=unroll),
        grid_spec=pltpu.PrefetchScalarGridSpec(
            num_scalar_prefetch=1,
            grid=(H, S // tq),
            in_specs=[
                pl.BlockSpec((1, tq, QK_DIM), lambda h, i, sc: (h, i, 0)),
                pl.BlockSpec((1, S, QK_DIM), lambda h, i, sc: (h, 0, 0)),
                pl.BlockSpec((1, S, V_HEAD), lambda h, i, sc: (h, 0, 0)),
            ],
            out_specs=pl.BlockSpec((tq, V_HEAD), lambda h, i, sc: (i, h)),
            scratch_shapes=[pltpu.VMEM((tq, 128), F32), pltpu.VMEM((tq, V_HEAD), F32), pltpu.VMEM((tq, 1), F32)],
        ),
        out_shape=jax.ShapeDtypeStruct((S, H * V_HEAD), BF16),
        compiler_params=_params("parallel", "arbitrary"),
        name="attention",
    )(sc, q, k, v)


def _conv_body(cur_ref, halo_ref, w_ref, b_ref, lg_ref, lb_ref, o_ref, ext_scr, sh_scr, conv_scr, *, tm, rb):
    halo = halo_ref[...].astype(F32)
    ext_scr[0:CONV_HALO, :] = jnp.where(pl.program_id(0) == 0, 0.0, halo)
    ext_scr[CONV_HALO:CONV_HALO + tm, :] = cur_ref[...].astype(F32)
    n_sh = sh_scr.shape[1]
    for r in range(1, SUBLANES):
        sh_scr[r - 1] = ext_scr[r:r + n_sh, :]
    base = CONV_HALO - (CONV_WIDTH - 1)
    for c0 in range(0, CONV_CH, 128):
        for r0 in range(0, tm, rb):
            acc = jnp.broadcast_to(b_ref[:, c0:c0 + 128], (rb, 128))
            for j in range(CONV_WIDTH):
                r = (base + j) % SUBLANES
                lo = r0 + base + j - r
                if r == 0:
                    rows = ext_scr[lo:lo + rb, c0:c0 + 128]
                else:
                    rows = sh_scr[r - 1, lo:lo + rb, c0:c0 + 128]
                acc = acc + rows * w_ref[j:j + 1, c0:c0 + 128]
            conv_scr[r0:r0 + rb, c0:c0 + 128] = acc
    a = conv_scr[...]
    xc = a - jnp.mean(a, axis=-1, keepdims=True)
    y = xc * lax.rsqrt(jnp.mean(xc * xc, axis=-1, keepdims=True) + EPS) * lg_ref[...] + lb_ref[...]
    o_ref[...] = (y * jax.nn.sigmoid(y)).astype(BF16)


def _conv(a, w, b, lg, lb, *, tm, rb):
    S, C = a.shape
    hb = tm // CONV_HALO
    row = lambda: pl.BlockSpec((1, C), lambda i: (0, 0))
    return pl.pallas_call(
        functools.partial(_conv_body, tm=tm, rb=rb),
        grid=(S // tm,),
        in_specs=[
            pl.BlockSpec((tm, C), lambda i: (i, 0)),
            pl.BlockSpec((CONV_HALO, C), lambda i: (jnp.maximum(i * hb - 1, 0), 0)),
            pl.BlockSpec((CONV_WIDTH, C), lambda i: (0, 0)),
            row(), row(), row(),
        ],
        out_specs=pl.BlockSpec((tm, C), lambda i: (i, 0)),
        out_shape=jax.ShapeDtypeStruct((S, C), BF16),
        scratch_shapes=[
            pltpu.VMEM((CONV_HALO + tm, C), F32),
            pltpu.VMEM((SUBLANES - 1, CONV_HALO + tm - SUBLANES, C), F32),
            pltpu.VMEM((tm, C), F32),
        ],
        compiler_params=_params("parallel"),
        name="conv",
    )(a, a, w, b, lg, lb)


def _mem_kv_body(mem_ref, g_ref, w_ref, gk_ref, mk_ref, mv_ref):
    n = (_rms(mem_ref[...]) * g_ref[...]).astype(BF16)
    kv = _dot(n, w_ref[...])
    width = MEM_HEADS * MEM_HEAD_DIM
    for hd in range(MEM_HEADS):
        lo = hd * MEM_HEAD_DIM
        mk_ref[:, lo:lo + MEM_HEAD_DIM] = (_rms(kv[:, lo:lo + MEM_HEAD_DIM]) * gk_ref[...]).astype(BF16)
    mv_ref[...] = kv[:, width:2 * width].astype(BF16)


def _mem_kv(mem, g, w, gk):
    M, D = mem.shape
    width = MEM_HEADS * MEM_HEAD_DIM
    full = lambda shape: pl.BlockSpec(shape, lambda i: (0,) * len(shape))
    return pl.pallas_call(
        _mem_kv_body,
        grid=(1,),
        in_specs=[full((M, D)), full((1, D)), full((D, 2 * width)), full((1, MEM_HEAD_DIM))],
        out_specs=[full((M, width)), full((M, width))],
        out_shape=[jax.ShapeDtypeStruct((M, width), BF16), jax.ShapeDtypeStruct((M, width), BF16)],
        compiler_params=_params("arbitrary"),
        name="mem_kv",
    )(mem, g, w, gk)


def _mem_attn_body(mq_ref, mk_ref, mv_ref, o_ref):
    for hd in range(MEM_HEADS):
        lo = hd * MEM_HEAD_DIM
        s = _dot_nt(mq_ref[:, lo:lo + MEM_HEAD_DIM], mk_ref[:, lo:lo + MEM_HEAD_DIM])
        p = jnp.exp(s - jnp.max(s, axis=-1, keepdims=True))
        p = p / jnp.sum(p, axis=-1, keepdims=True)
        o_ref[:, lo:lo + MEM_HEAD_DIM] = _dot(p.astype(BF16), mv_ref[:, lo:lo + MEM_HEAD_DIM]).astype(BF16)


def _mem_attn(mq, mk, mv, *, tm):
    S, W = mq.shape
    M = mk.shape[0]
    return pl.pallas_call(
        _mem_attn_body,
        grid=(S // tm,),
        in_specs=[
            pl.BlockSpec((tm, W), lambda i: (i, 0)),
            pl.BlockSpec((M, W), lambda i: (0, 0)),
            pl.BlockSpec((M, W), lambda i: (0, 0)),
        ],
        out_specs=pl.BlockSpec((tm, W), lambda i: (i, 0)),
        out_shape=jax.ShapeDtypeStruct((S, W), BF16),
        compiler_params=_params("parallel"),
        name="mem_attn",
    )(mq, mk, mv)


def _resident(shape):
    return pl.BlockSpec(shape, lambda i: (0,) * len(shape), pipeline_mode=pl.Buffered(1))


def _merge_body(c_ref, o_ref, om_ref, g0_ref, g1_ref, g2_ref, wc_ref, wo_ref, wm_ref, out_ref, *, tn):
    c, o, om = c_ref[...], o_ref[...], om_ref[...]
    for n0 in range(0, out_ref.shape[1], tn):
        cols = slice(n0, n0 + tn)
        out_ref[:, cols] = (g0_ref[:, cols].astype(F32) * _dot(c, wc_ref[:, cols])
                            + g1_ref[:, cols].astype(F32) * _dot(o, wo_ref[:, cols])
                            + g2_ref[:, cols].astype(F32) * _dot(om, wm_ref[:, cols])).astype(BF16)


def _merge(c, o, om, gates, wc, wo, wm, *, tm, tn):
    S = c.shape[0]
    D = wc.shape[1]
    rows = lambda a: pl.BlockSpec((tm, a.shape[1]), lambda i: (i, 0))
    return pl.pallas_call(
        functools.partial(_merge_body, tn=tn),
        grid=(S // tm,),
        in_specs=[
            rows(c), rows(o), rows(om),
            pl.BlockSpec((tm, D), lambda i: (i, 0)),
            pl.BlockSpec((tm, D), lambda i: (i, 1)),
            pl.BlockSpec((tm, D), lambda i: (i, 2)),
            _resident(wc.shape), _resident(wo.shape), _resident(wm.shape),
        ],
        out_specs=pl.BlockSpec((tm, D), lambda i: (i, 0)),
        out_shape=jax.ShapeDtypeStruct((S, D), BF16),
        compiler_params=_params("parallel"),
        name="merge",
    )(c, o, om, gates, gates, gates, wc, wo, wm)


def _out_proj_body(x_ref, m_ref, w_ref, out_ref, *, tn):
    m = m_ref[...]
    for n0 in range(0, out_ref.shape[1], tn):
        cols = slice(n0, n0 + tn)
        out_ref[:, cols] = x_ref[:, cols] + _dot(m, w_ref[:, cols])


def _out_proj(x, m, w, *, tm, tn):
    S, D = x.shape
    return pl.pallas_call(
        functools.partial(_out_proj_body, tn=tn),
        grid=(S // tm,),
        in_specs=[
            pl.BlockSpec((tm, D), lambda i: (i, 0)),
            pl.BlockSpec((tm, D), lambda i: (i, 0)),
            _resident(w.shape),
        ],
        out_specs=pl.BlockSpec((tm, D), lambda i: (i, 0)),
        out_shape=jax.ShapeDtypeStruct((S, D), F32),
        compiler_params=_params("parallel"),
        name="out_proj",
    )(x, m, w)


def _swap_halves(v):
    return jnp.concatenate([v[..., HALF_ROPE:], v[..., :HALF_ROPE]], axis=-1)


def _tile(S, want):
    return min(S, want)


def _layer(x, mem, pos, p, l):
    S = x.shape[0]
    row = lambda v: v.reshape(1, -1).astype(F32)
    bf = lambda w: w.astype(BF16)

    inv_freq = ROPE_THETA ** (-jnp.arange(0, QK_ROPE, 2, dtype=F32) / QK_ROPE)
    freq64 = jnp.concatenate([inv_freq, inv_freq])
    sign64 = jnp.concatenate([-jnp.ones((HALF_ROPE,), F32), jnp.ones((HALF_ROPE,), F32)])
    zeros64 = jnp.zeros((QK_ROPE,), F32)

    w_in = p["w_in"][l]
    c0 = 2 * CONV_CH
    c1 = c0 + Q_LORA
    c2 = c1 + KV_LORA
    c3 = c2 + QK_ROPE
    c4 = c3 + MEM_HEADS * MEM_HEAD_DIM
    w_kpe = w_in[:, c2:c3]
    w_small = bf(jnp.concatenate(
        [w_in[:, c0:c1], w_in[:, c1:c2], w_in[:, c3:c4], w_kpe, w_kpe, _swap_halves(w_kpe), jnp.zeros_like(w_kpe)],
        axis=1))

    w_uq = p["w_uq"][l].reshape(Q_LORA, MLA_HEADS, QK_DIM).transpose(1, 0, 2)
    wq = bf(jnp.concatenate([w_uq, _swap_halves(w_uq[..., QK_NOPE:])], axis=-1))
    w_kv = bf(p["w_ukv"][l].reshape(KV_LORA, MLA_HEADS, QK_NOPE + V_HEAD).transpose(1, 0, 2))

    g_q = p["g_qnorm"][l]
    g_k = p["g_knorm"][l]
    gq_rope, gk_rope = g_q[QK_NOPE:], g_k[QK_NOPE:]
    k_gc = jnp.concatenate([gk_rope, jnp.ones((QK_ROPE,), F32)])
    k_gs = jnp.concatenate([sign64 * _swap_halves(gk_rope), zeros64])
    q_g = jnp.concatenate([gq_rope, sign64 * _swap_halves(gq_rope)])
    freq128 = jnp.concatenate([freq64, freq64])

    x1, h = _ffn(x, row(p["g_ffn1"][l]), bf(p["w_ffn1_gu"][l]), bf(p["w_ffn1_down"][l]), row(p["g_mix"][l]),
                 emit_norm=True, tm=_tile(S, 512), tf=512)

    gates = _gates(h, w_in, row(p["b_gate"][l]), tm=_tile(S, 1024), tn=1024, col0=c4)
    a = _glu(h, w_in, tm=_tile(S, 1024), tn=512)
    cq, ckv, mq, kaux, qtab = _small_proj(
        h, w_small, pos, row(p["g_q_a"][l]), row(p["g_kv_a"][l]), row(p["g_mqnorm"][l]),
        row(freq128), row(k_gc), row(k_gs), row(q_g), tm=_tile(S, 512), col0=0)

    conv = _conv(a, p["conv_w"][l].astype(F32), row(p["conv_b"][l]), row(p["conv_ln_g"][l]),
                 row(p["conv_ln_b"][l]), tm=_tile(S, 256), rb=64)

    q, k, v = _qkv_prep(cq, ckv, kaux, qtab, wq, w_kv, row(g_q[:QK_NOPE]), row(g_k[:QK_NOPE]), tm=_tile(S, 512))
    bound = (QK_DIM ** 0.5 * LOG2E * 1.01) * jnp.max(jnp.abs(g_q)) * jnp.max(jnp.abs(g_k))
    sc = jnp.stack([bound, (bound <= MAX_STABLE_BOUND_LOG2).astype(F32)])
    o = _attention(sc, q, k, v, tq=_tile(S, 1024), tk=512, unroll=4)

    mk, mv = _mem_kv(mem, row(p["g_mem"][l]), bf(p["w_mem_kv"][l]), row(p["g_mknorm"][l]))
    om = _mem_attn(mq, mk, mv, tm=_tile(S, 512))

    merged = _merge(conv, o, om, gates, bf(p["w_conv_out"][l]), bf(p["w_mla_out"][l]), bf(p["w_mem_out"][l]),
                    tm=_tile(S, 512), tn=512)
    x2 = _out_proj(x1, merged, bf(p["w_out"][l]), tm=_tile(S, 512), tn=512)

    (x3,) = _ffn(x2, row(p["g_ffn2"][l]), bf(p["w_ffn2_gu"][l]), bf(p["w_ffn2_down"][l]), row(p["g_ffn2"][l]),
                 emit_norm=False, tm=_tile(S, 512), tf=512)
    return x3


def kernel(x, mem, positions, g_ffn1, w_ffn1_gu, w_ffn1_down, g_mix, w_in, b_gate, conv_w, conv_b, conv_ln_g, conv_ln_b, w_conv_out, g_q_a, w_uq, g_kv_a, w_ukv, g_qnorm, g_knorm, w_mla_out, g_mem, w_mem_kv, g_mqnorm, g_mknorm, w_mem_out, w_out, g_ffn2, w_ffn2_gu, w_ffn2_down):
    p = dict(g_ffn1=g_ffn1, w_ffn1_gu=w_ffn1_gu, w_ffn1_down=w_ffn1_down, g_mix=g_mix, w_in=w_in, b_gate=b_gate,
             conv_w=conv_w, conv_b=conv_b, conv_ln_g=conv_ln_g, conv_ln_b=conv_ln_b, w_conv_out=w_conv_out,
             g_q_a=g_q_a, w_uq=w_uq, g_kv_a=g_kv_a, w_ukv=w_ukv, g_qnorm=g_qnorm, g_knorm=g_knorm,
             w_mla_out=w_mla_out, g_mem=g_mem, w_mem_kv=w_mem_kv, g_mqnorm=g_mqnorm, g_mknorm=g_mknorm,
             w_mem_out=w_mem_out, w_out=w_out, g_ffn2=g_ffn2, w_ffn2_gu=w_ffn2_gu, w_ffn2_down=w_ffn2_down)
    B, S, D = x.shape
    depth = g_ffn1.shape[0]
    outs = []
    for b in range(B):
        xb = x.reshape(S, D) if B == 1 else x[b]
        pos = positions[b].reshape(S, 1)
        for l in range(depth):
            xb = _layer(xb, mem[b], pos, p, l)
        outs.append(xb.reshape(1, S, D))
    return outs[0] if B == 1 else jnp.concatenate(outs, axis=0)
```

```python
import functools

import jax
import jax.numpy as jnp
from jax import lax
from jax.experimental import pallas as pl
from jax.experimental.pallas import tpu as pltpu

F32 = jnp.float32
BF16 = jnp.bfloat16

D_MODEL = 2048
MEM_LEN = 256
D_FF = 5632
CONV_CH = 1536
CONV_WIDTH = 31
MLA_HEADS = 12
Q_LORA = 512
KV_LORA = 512
QK_NOPE = 128
QK_ROPE = 64
QK_DIM = QK_NOPE + QK_ROPE
V_HEAD = 128
MEM_HEADS = 4
MEM_HEAD_DIM = 256
N_BRANCH = 3
ROPE_THETA = 10000.0
EPS = 1e-6
HALF_ROPE = QK_ROPE // 2
LOG2E = 1.4426950408889634
MAX_STABLE_BOUND_LOG2 = 30.0 * LOG2E

VMEM_LIMIT_BYTES = 56 * 1024 * 1024
SUBLANES = 8
LANES = 128
CONV_HALO = 32


def _params(*sem):
    return pltpu.CompilerParams(dimension_semantics=sem, vmem_limit_bytes=VMEM_LIMIT_BYTES)


def _rms(x):
    return x * lax.rsqrt(jnp.mean(x * x, axis=-1, keepdims=True) + EPS)


def _dot(a, b):
    return jnp.dot(a, b, preferred_element_type=F32)


def _dot_nt(a, b):
    return lax.dot_general(a, b, (((1,), (1,)), ((), ())), preferred_element_type=F32)


def _ffn_body(x_ref, g_ref, wg_ref, wu_ref, wd_ref, gn_ref, o_ref, *rest, emit_norm):
    if emit_norm:
        h_ref, n_scr = rest
    else:
        (n_scr,) = rest
    j = pl.program_id(1)

    @pl.when(j == 0)
    def _():
        x = x_ref[...]
        n_scr[...] = (_rms(x) * g_ref[...]).astype(BF16)
        o_ref[...] = x

    n = n_scr[...]
    gg = _dot(n, wg_ref[...])
    uu = _dot(n, wu_ref[...])
    act = (gg * jax.nn.sigmoid(gg) * uu * 0.5).astype(BF16)
    o_ref[...] += _dot(act, wd_ref[...])

    if emit_norm:

        @pl.when(j == pl.num_programs(1) - 1)
        def _():
            h_ref[...] = (_rms(o_ref[...]) * gn_ref[...]).astype(BF16)


def _ffn(x, g, w_gu, w_down, g_next, *, emit_norm, tm, tf):
    S, D = x.shape
    nf = D_FF // tf
    out_shape = [jax.ShapeDtypeStruct((S, D), F32)]
    out_specs = [pl.BlockSpec((tm, D), lambda i, j: (i, 0))]
    if emit_norm:
        out_shape.append(jax.ShapeDtypeStruct((S, D), BF16))
        out_specs.append(pl.BlockSpec((tm, D), lambda i, j: (i, 0)))
    return pl.pallas_call(
        functools.partial(_ffn_body, emit_norm=emit_norm),
        grid=(S // tm, nf),
        in_specs=[
            pl.BlockSpec((tm, D), lambda i, j: (i, 0)),
            pl.BlockSpec((1, D), lambda i, j: (0, 0)),
            pl.BlockSpec((D, tf), lambda i, j: (0, j)),
            pl.BlockSpec((D, tf), lambda i, j: (0, j + nf)),
            pl.BlockSpec((tf, D), lambda i, j: (j, 0)),
            pl.BlockSpec((1, D), lambda i, j: (0, 0)),
        ],
        out_specs=out_specs,
        out_shape=out_shape,
        scratch_shapes=[pltpu.VMEM((tm, D), BF16)],
        compiler_params=_params("parallel", "arbitrary"),
        name="ffn_norm" if emit_norm else "ffn",
    )(x, g, w_gu, w_gu, w_down, g_next)


def _gates_body(h_ref, w_ref, wnext_ref, b_ref, o_ref, w_scr, *, shift):
    @pl.when(pl.program_id(1) == 0)
    def _():
        keep = w_scr.shape[0] - shift
        w_scr[0:keep, :] = w_ref[shift:, :].astype(BF16)
        if shift:
            w_scr[keep:, :] = wnext_ref[...].astype(BF16)

    o_ref[...] = jax.nn.sigmoid(_dot_nt(h_ref[...], w_scr[...]) + b_ref[...]).astype(BF16)


def _gates(h, wt, b, *, tm, tn, row0):
    S, D = h.shape
    N = b.shape[1]
    shift = row0 % tn
    j0 = row0 // tn
    assert shift % 16 == 0 and tn % max(shift, 1) == 0 and (row0 + N) % max(shift, 1) == 0
    halo = shift if shift else tn
    per = tn // halo
    return pl.pallas_call(
        functools.partial(_gates_body, shift=shift),
        grid=(N // tn, S // tm),
        in_specs=[
            pl.BlockSpec((tm, D), lambda j, i: (i, 0)),
            pl.BlockSpec((tn, D), lambda j, i: (j + j0, 0)),
            pl.BlockSpec((halo, D), lambda j, i: ((j + j0 + 1) * per, 0)),
            pl.BlockSpec((1, tn), lambda j, i: (0, j)),
        ],
        out_specs=pl.BlockSpec((tm, tn), lambda j, i: (i, j)),
        out_shape=jax.ShapeDtypeStruct((S, N), BF16),
        scratch_shapes=[pltpu.VMEM((tn, D), BF16)],
        compiler_params=_params("parallel", "arbitrary"),
        name="gates",
    )(h, wt, wt, b)


def _glu_body(h_ref, wa_ref, ws_ref, o_ref, wa_scr, ws_scr):
    @pl.when(pl.program_id(1) == 0)
    def _():
        wa_scr[...] = wa_ref[...].astype(BF16)
        ws_scr[...] = ws_ref[...].astype(BF16)

    h = h_ref[...]
    o_ref[...] = (_dot_nt(h, wa_scr[...]) * jax.nn.sigmoid(_dot_nt(h, ws_scr[...]))).astype(BF16)


def _glu(h, wt, *, tm, tn):
    S, D = h.shape
    nn = CONV_CH // tn
    return pl.pallas_call(
        _glu_body,
        grid=(nn, S // tm),
        in_specs=[
            pl.BlockSpec((tm, D), lambda j, i: (i, 0)),
            pl.BlockSpec((tn, D), lambda j, i: (j, 0)),
            pl.BlockSpec((tn, D), lambda j, i: (j + nn, 0)),
        ],
        out_specs=pl.BlockSpec((tm, tn), lambda j, i: (i, j)),
        out_shape=jax.ShapeDtypeStruct((S, CONV_CH), BF16),
        scratch_shapes=[pltpu.VMEM((tn, D), BF16), pltpu.VMEM((tn, D), BF16)],
        compiler_params=_params("parallel", "arbitrary"),
        name="glu",
    )(h, wt, wt)


_SP_CQ = 0
_SP_CKV = _SP_CQ + Q_LORA
_SP_MQ = _SP_CKV + KV_LORA
_SP_P1 = _SP_MQ + MEM_HEADS * MEM_HEAD_DIM
_SP_P2 = _SP_P1 + 2 * QK_ROPE
_SP_COLS = _SP_P2 + 2 * QK_ROPE


def _small_proj_body(h_ref, w_ref, pos_ref, gqa_ref, gkva_ref, gmq_ref, freq_ref, kgc_ref, kgs_ref, qg_ref,
                     cq_ref, ckv_ref, mq_ref, kaux_ref, qtab_ref):
    r = _dot(h_ref[...], w_ref[...])
    cq_ref[...] = (_rms(r[:, _SP_CQ:_SP_CKV]) * gqa_ref[...]).astype(BF16)
    ckv_ref[...] = (_rms(r[:, _SP_CKV:_SP_MQ]) * gkva_ref[...]).astype(BF16)
    for hd in range(MEM_HEADS):
        lo = hd * MEM_HEAD_DIM
        m = r[:, _SP_MQ + lo:_SP_MQ + lo + MEM_HEAD_DIM]
        mq_ref[:, lo:lo + MEM_HEAD_DIM] = (_rms(m) * gmq_ref[...] * (MEM_HEAD_DIM ** -0.5)).astype(BF16)
    ang = pos_ref[...].astype(F32) * freq_ref[...]
    cos, sin = jnp.cos(ang), jnp.sin(ang)
    low = lax.broadcasted_iota(jnp.int32, ang.shape, 1) < QK_ROPE
    kaux_ref[...] = (r[:, _SP_P1:_SP_P2] * (kgc_ref[...] * jnp.where(low, cos, 1.0))
                     + r[:, _SP_P2:_SP_COLS] * (kgs_ref[...] * sin))
    qtab_ref[...] = qg_ref[...] * jnp.where(low, cos, sin)


def _small_proj(h, w, pos, gqa, gkva, gmq, freq, kgc, kgs, qg, *, tm, col0):
    S, D = h.shape
    row = lambda n: pl.BlockSpec((1, n), lambda i: (0, 0))
    return pl.pallas_call(
        _small_proj_body,
        grid=(S // tm,),
        in_specs=[
            pl.BlockSpec((tm, D), lambda i: (i, 0)),
            pl.BlockSpec((D, _SP_COLS), lambda i: (0, col0 // _SP_COLS)),
            pl.BlockSpec((tm, 1), lambda i: (i, 0)),
            row(Q_LORA), row(KV_LORA), row(MEM_HEAD_DIM),
            row(2 * QK_ROPE), row(2 * QK_ROPE), row(2 * QK_ROPE), row(2 * QK_ROPE),
        ],
        out_specs=[
            pl.BlockSpec((tm, Q_LORA), lambda i: (i, 0)),
            pl.BlockSpec((tm, KV_LORA), lambda i: (i, 0)),
            pl.BlockSpec((tm, MEM_HEADS * MEM_HEAD_DIM), lambda i: (i, 0)),
            pl.BlockSpec((tm, 2 * QK_ROPE), lambda i: (i, 0)),
            pl.BlockSpec((tm, 2 * QK_ROPE), lambda i: (i, 0)),
        ],
        out_shape=[
            jax.ShapeDtypeStruct((S, Q_LORA), BF16),
            jax.ShapeDtypeStruct((S, KV_LORA), BF16),
            jax.ShapeDtypeStruct((S, MEM_HEADS * MEM_HEAD_DIM), BF16),
            jax.ShapeDtypeStruct((S, 2 * QK_ROPE), F32),
            jax.ShapeDtypeStruct((S, 2 * QK_ROPE), F32),
        ],
        compiler_params=_params("parallel"),
        name="small_proj",
    )(h, w, pos, gqa, gkva, gmq, freq, kgc, kgs, qg)


def _qkv_body(cq_ref, ckv_ref, kaux_ref, qtab_ref, wq_ref, wkv_ref, gqn_ref, gkn_ref, q_ref, k_ref, v_ref):
    cq = cq_ref[...]
    ckv = ckv_ref[...]
    kaux = kaux_ref[...]
    qtab = qtab_ref[...]
    low = lax.broadcasted_iota(jnp.int32, kaux.shape, 1) < QK_ROPE
    kss = jnp.sum(jnp.where(low, 0.0, kaux * kaux), axis=-1, keepdims=True)
    k_rope = kaux[:, 0:QK_ROPE]
    nxt = (_dot(cq, wq_ref[0]), _dot(ckv, wkv_ref[0]))
    for hd in range(MLA_HEADS):
        qf, kv = nxt
        if hd + 1 < MLA_HEADS:
            nxt = (_dot(cq, wq_ref[hd + 1]), _dot(ckv, wkv_ref[hd + 1]))
        nope, rot = qf[:, 0:QK_NOPE], qf[:, QK_NOPE:]
        ss = jnp.sum(nope * nope + jnp.where(low, rot * rot, 0.0), axis=-1, keepdims=True)
        rq = lax.rsqrt(ss * (1.0 / QK_DIM) + EPS) * (QK_DIM ** -0.5 * LOG2E)
        q_ref[hd, :, 0:QK_NOPE] = (nope * rq * gqn_ref[...]).astype(BF16)
        mixed = rot * qtab
        mixed = mixed + pltpu.roll(mixed, QK_ROPE, axis=1)
        q_ref[hd, :, QK_NOPE:QK_DIM] = (mixed[:, 0:QK_ROPE] * rq).astype(BF16)

        kn = kv[:, 0:QK_NOPE]
        rk = lax.rsqrt((jnp.sum(kn * kn, axis=-1, keepdims=True) + kss) * (1.0 / QK_DIM) + EPS)
        k_ref[hd, :, 0:QK_NOPE] = (kn * rk * gkn_ref[...]).astype(BF16)
        k_ref[hd, :, QK_NOPE:QK_DIM] = (k_rope * rk).astype(BF16)
        v_ref[hd] = kv[:, QK_NOPE:QK_NOPE + V_HEAD].astype(BF16)


def _qkv_prep(cq, ckv, kaux, qtab, wq, wkv, gqn, gkn, *, tm):
    S = cq.shape[0]
    H = MLA_HEADS
    row = lambda n: pl.BlockSpec((1, n), lambda i: (0, 0))
    return pl.pallas_call(
        _qkv_body,
        grid=(S // tm,),
        in_specs=[
            pl.BlockSpec((tm, Q_LORA), lambda i: (i, 0)),
            pl.BlockSpec((tm, KV_LORA), lambda i: (i, 0)),
            pl.BlockSpec((tm, 2 * QK_ROPE), lambda i: (i, 0)),
            pl.BlockSpec((tm, 2 * QK_ROPE), lambda i: (i, 0)),
            pl.BlockSpec((H, Q_LORA, QK_DIM + QK_ROPE), lambda i: (0, 0, 0)),
            pl.BlockSpec((H, KV_LORA, QK_NOPE + V_HEAD), lambda i: (0, 0, 0)),
            row(QK_NOPE), row(QK_NOPE),
        ],
        out_specs=[
            pl.BlockSpec((H, tm, QK_DIM), lambda i: (0, i, 0)),
            pl.BlockSpec((H, tm, QK_DIM), lambda i: (0, i, 0)),
            pl.BlockSpec((H, tm, V_HEAD), lambda i: (0, i, 0)),
        ],
        out_shape=[
            jax.ShapeDtypeStruct((H, S, QK_DIM), BF16),
            jax.ShapeDtypeStruct((H, S, QK_DIM), BF16),
            jax.ShapeDtypeStruct((H, S, V_HEAD), BF16),
        ],
        compiler_params=_params("parallel"),
        name="qkv_prep",
    )(cq, ckv, kaux, qtab, wq, wkv, gqn, gkn)


def _attn_body(sc_ref, q_ref, k_ref, v_ref, o_ref, l_scr, acc_scr, m_scr, *, tq, tk, unroll):
    qi = pl.program_id(1)
    q = q_ref[0]
    n_diag = tq // tk
    n_full = qi * n_diag

    def kv_block(kb):
        start = pl.multiple_of(kb * tk, tk)
        return k_ref[0, pl.ds(start, tk), :], v_ref[0, pl.ds(start, tk), :]

    def causal(x, kb, fill, row0=0):
        row = lax.broadcasted_iota(jnp.int32, x.shape, 0) + (qi * tq + row0)
        col = lax.broadcasted_iota(jnp.int32, x.shape, 1) + kb * tk
        return jnp.where(col <= row, x, fill)

    @pl.when(sc_ref[1] > 0.0)
    def _bounded():
        bound = sc_ref[0]

        def block(kb, masked, row0=0):
            k, v = kv_block(kb)
            p = jnp.exp2(_dot_nt(q[row0:], k) - bound)
            if masked:
                p = causal(p, kb, 0.0, row0)
            lanes = p[:, 0:128]
            for c in range(128, tk, 128):
                lanes = lanes + p[:, c:c + 128]
            return lanes, _dot(p.astype(BF16), v)

        def add(kb, n, masked=False):
            lanes, pv = block(kb, masked)
            for u in range(1, n):
                lanes_u, pv_u = block(kb + u, masked)
                lanes, pv = lanes + lanes_u, pv + pv_u
            return lanes, pv

        l_scr[...] = jnp.zeros(l_scr.shape, F32)
        acc_scr[...] = jnp.zeros(acc_scr.shape, F32)

        def many(sb, carry):
            lanes, pv = add(sb * unroll, unroll)
            l_scr[...] += lanes
            acc_scr[...] += pv
            return carry

        def one(kb, carry):
            lanes, pv = add(kb, 1)
            l_scr[...] += lanes
            acc_scr[...] += pv
            return carry

        n_many = n_full // unroll
        lax.fori_loop(0, n_many, many, 0)
        lax.fori_loop(n_many * unroll, n_full, one, 0)
        for d in range(n_diag):
            lanes, pv = block(n_full + d, True, row0=d * tk)
            l_scr[d * tk:, :] += lanes
            acc_scr[d * tk:, :] += pv
        l = jnp.sum(l_scr[...], axis=-1, keepdims=True)
        o_ref[...] = (acc_scr[...] / l).astype(BF16)

    @pl.when(sc_ref[1] <= 0.0)
    def _online():
        m_scr[...] = jnp.full(m_scr.shape, -jnp.inf, F32)
        l_scr[...] = jnp.zeros(l_scr.shape, F32)
        acc_scr[...] = jnp.zeros(acc_scr.shape, F32)

        def step(kb, masked):
            k, v = kv_block(kb)
            s = _dot_nt(q, k)
            if masked:
                s = causal(s, kb, -jnp.inf)
            m_old = m_scr[...]
            m_new = jnp.maximum(m_old, jnp.max(s, axis=-1, keepdims=True))
            p = jnp.exp2(s - m_new)
            alpha = jnp.exp2(m_old - m_new)
            l_scr[:, 0:1] = alpha * l_scr[:, 0:1] + jnp.sum(p, axis=-1, keepdims=True)
            acc_scr[...] = alpha * acc_scr[...] + _dot(p.astype(BF16), v)
            m_scr[...] = m_new

        def body(kb, carry):
            step(kb, False)
            return carry

        lax.fori_loop(0, n_full, body, 0)
        for d in range(n_diag):
            step(n_full + d, True)
        o_ref[...] = (acc_scr[...] / l_scr[:, 0:1]).astype(BF16)


def _attention(sc, q, k, v, *, tq, tk, unroll):
    H, S, _ = q.shape
    return pl.pallas_call(
        functools.partial(_attn_body, tq=tq, tk=tk, unroll=unroll),
        grid_spec=pltpu.PrefetchScalarGridSpec(
            num_scalar_prefetch=1,
            grid=(H, S // tq),
            in_specs=[
                pl.BlockSpec((1, tq, QK_DIM), lambda h, i, sc: (h, i, 0)),
                pl.BlockSpec((1, S, QK_DIM), lambda h, i, sc: (h, 0, 0)),
                pl.BlockSpec((1, S, V_HEAD), lambda h, i, sc: (h, 0, 0)),
            ],
            out_specs=pl.BlockSpec((tq, V_HEAD), lambda h, i, sc: (i, h)),
            scratch_shapes=[pltpu.VMEM((tq, 128), F32), pltpu.VMEM((tq, V_HEAD), F32), pltpu.VMEM((tq, 1), F32)],
        ),
        out_shape=jax.ShapeDtypeStruct((S, H * V_HEAD), BF16),
        compiler_params=_params("parallel", "arbitrary"),
        name="attention",
    )(sc, q, k, v)


def _conv_body(cur_ref, halo_ref, w_ref, b_ref, lg_ref, lb_ref, o_ref, ext_scr, sh_scr, conv_scr, *, tm, rb):
    halo = halo_ref[...].astype(F32)
    ext_scr[0:CONV_HALO, :] = jnp.where(pl.program_id(0) == 0, 0.0, halo)
    ext_scr[CONV_HALO:CONV_HALO + tm, :] = cur_ref[...].astype(F32)
    n_sh = sh_scr.shape[1]
    for r in range(1, SUBLANES):
        sh_scr[r - 1] = ext_scr[r:r + n_sh, :]
    base = CONV_HALO - (CONV_WIDTH - 1)
    for c0 in range(0, CONV_CH, 128):
        for r0 in range(0, tm, rb):
            acc = jnp.broadcast_to(b_ref[:, c0:c0 + 128], (rb, 128))
            for j in range(CONV_WIDTH):
                r = (base + j) % SUBLANES
                lo = r0 + base + j - r
                if r == 0:
                    rows = ext_scr[lo:lo + rb, c0:c0 + 128]
                else:
                    rows = sh_scr[r - 1, lo:lo + rb, c0:c0 + 128]
                acc = acc + rows * w_ref[j:j + 1, c0:c0 + 128]
            conv_scr[r0:r0 + rb, c0:c0 + 128] = acc
    a = conv_scr[...]
    xc = a - jnp.mean(a, axis=-1, keepdims=True)
    y = xc * lax.rsqrt(jnp.mean(xc * xc, axis=-1, keepdims=True) + EPS) * lg_ref[...] + lb_ref[...]
    o_ref[...] = (y * jax.nn.sigmoid(y)).astype(BF16)


def _conv(a, w, b, lg, lb, *, tm, rb):
    S, C = a.shape
    hb = tm // CONV_HALO
    row = lambda: pl.BlockSpec((1, C), lambda i: (0, 0))
    return pl.pallas_call(
        functools.partial(_conv_body, tm=tm, rb=rb),
        grid=(S // tm,),
        in_specs=[
            pl.BlockSpec((tm, C), lambda i: (i, 0)),
            pl.BlockSpec((CONV_HALO, C), lambda i: (jnp.maximum(i * hb - 1, 0), 0)),
            pl.BlockSpec((CONV_WIDTH, C), lambda i: (0, 0)),
            row(), row(), row(),
        ],
        out_specs=pl.BlockSpec((tm, C), lambda i: (i, 0)),
        out_shape=jax.ShapeDtypeStruct((S, C), BF16),
        scratch_shapes=[
            pltpu.VMEM((CONV_HALO + tm, C), F32),
            pltpu.VMEM((SUBLANES - 1, CONV_HALO + tm - SUBLANES, C), F32),
            pltpu.VMEM((tm, C), F32),
        ],
        compiler_params=_params("parallel"),
        name="conv",
    )(a, a, w, b, lg, lb)


def _mem_kv_body(mem_ref, g_ref, w_ref, gk_ref, mk_ref, mv_ref):
    n = (_rms(mem_ref[...]) * g_ref[...]).astype(BF16)
    kv = _dot(n, w_ref[...])
    width = MEM_HEADS * MEM_HEAD_DIM
    for hd in range(MEM_HEADS):
        lo = hd * MEM_HEAD_DIM
        mk_ref[:, lo:lo + MEM_HEAD_DIM] = (_rms(kv[:, lo:lo + MEM_HEAD_DIM]) * gk_ref[...]).astype(BF16)
    mv_ref[...] = kv[:, width:2 * width].astype(BF16)


def _mem_kv(mem, g, w, gk):
    M, D = mem.shape
    width = MEM_HEADS * MEM_HEAD_DIM
    full = lambda shape: pl.BlockSpec(shape, lambda i: (0,) * len(shape))
    return pl.pallas_call(
        _mem_kv_body,
        grid=(1,),
        in_specs=[full((M, D)), full((1, D)), full((D, 2 * width)), full((1, MEM_HEAD_DIM))],
        out_specs=[full((M, width)), full((M, width))],
        out_shape=[jax.ShapeDtypeStruct((M, width), BF16), jax.ShapeDtypeStruct((M, width), BF16)],
        compiler_params=_params("arbitrary"),
        name="mem_kv",
    )(mem, g, w, gk)


def _mem_attn_body(mq_ref, mk_ref, mv_ref, o_ref):
    for hd in range(MEM_HEADS):
        lo = hd * MEM_HEAD_DIM
        s = _dot_nt(mq_ref[:, lo:lo + MEM_HEAD_DIM], mk_ref[:, lo:lo + MEM_HEAD_DIM])
        p = jnp.exp(s - jnp.max(s, axis=-1, keepdims=True))
        p = p / jnp.sum(p, axis=-1, keepdims=True)
        o_ref[:, lo:lo + MEM_HEAD_DIM] = _dot(p.astype(BF16), mv_ref[:, lo:lo + MEM_HEAD_DIM]).astype(BF16)


def _mem_attn(mq, mk, mv, *, tm):
    S, W = mq.shape
    M = mk.shape[0]
    return pl.pallas_call(
        _mem_attn_body,
        grid=(S // tm,),
        in_specs=[
            pl.BlockSpec((tm, W), lambda i: (i, 0)),
            pl.BlockSpec((M, W), lambda i: (0, 0)),
            pl.BlockSpec((M, W), lambda i: (0, 0)),
        ],
        out_specs=pl.BlockSpec((tm, W), lambda i: (i, 0)),
        out_shape=jax.ShapeDtypeStruct((S, W), BF16),
        compiler_params=_params("parallel"),
        name="mem_attn",
    )(mq, mk, mv)


def _resident(shape):
    return pl.BlockSpec(shape, lambda i: (0,) * len(shape), pipeline_mode=pl.Buffered(1))


def _merge_body(c_ref, o_ref, om_ref, g0_ref, g1_ref, g2_ref, wc_ref, wo_ref, wm_ref, out_ref, *, tn):
    c, o, om = c_ref[...], o_ref[...], om_ref[...]
    for n0 in range(0, out_ref.shape[1], tn):
        cols = slice(n0, n0 + tn)
        out_ref[:, cols] = (g0_ref[:, cols].astype(F32) * _dot(c, wc_ref[:, cols])
                            + g1_ref[:, cols].astype(F32) * _dot(o, wo_ref[:, cols])
                            + g2_ref[:, cols].astype(F32) * _dot(om, wm_ref[:, cols])).astype(BF16)


def _merge(c, o, om, gates, wc, wo, wm, *, tm, tn):
    S = c.shape[0]
    D = wc.shape[1]
    rows = lambda a: pl.BlockSpec((tm, a.shape[1]), lambda i: (i, 0))
    return pl.pallas_call(
        functools.partial(_merge_body, tn=tn),
        grid=(S // tm,),
        in_specs=[
            rows(c), rows(o), rows(om),
            pl.BlockSpec((tm, D), lambda i: (i, 0)),
            pl.BlockSpec((tm, D), lambda i: (i, 1)),
            pl.BlockSpec((tm, D), lambda i: (i, 2)),
            _resident(wc.shape), _resident(wo.shape), _resident(wm.shape),
        ],
        out_specs=pl.BlockSpec((tm, D), lambda i: (i, 0)),
        out_shape=jax.ShapeDtypeStruct((S, D), BF16),
        compiler_params=_params("parallel"),
        name="merge",
    )(c, o, om, gates, gates, gates, wc, wo, wm)


def _out_proj_body(x_ref, m_ref, w_ref, out_ref, *, tn):
    m = m_ref[...]
    for n0 in range(0, out_ref.shape[1], tn):
        cols = slice(n0, n0 + tn)
        out_ref[:, cols] = x_ref[:, cols] + _dot(m, w_ref[:, cols])


def _out_proj(x, m, w, *, tm, tn):
    S, D = x.shape
    return pl.pallas_call(
        functools.partial(_out_proj_body, tn=tn),
        grid=(S // tm,),
        in_specs=[
            pl.BlockSpec((tm, D), lambda i: (i, 0)),
            pl.BlockSpec((tm, D), lambda i: (i, 0)),
            _resident(w.shape),
        ],
        out_specs=pl.BlockSpec((tm, D), lambda i: (i, 0)),
        out_shape=jax.ShapeDtypeStruct((S, D), F32),
        compiler_params=_params("parallel"),
        name="out_proj",
    )(x, m, w)


def _swap_halves(v):
    return jnp.concatenate([v[..., HALF_ROPE:], v[..., :HALF_ROPE]], axis=-1)


def _tile(S, want):
    return min(S, want)


def _layer(x, mem, pos, p, l):
    S = x.shape[0]
    row = lambda v: v.reshape(1, -1).astype(F32)
    bf = lambda w: w.astype(BF16)

    inv_freq = ROPE_THETA ** (-jnp.arange(0, QK_ROPE, 2, dtype=F32) / QK_ROPE)
    freq64 = jnp.concatenate([inv_freq, inv_freq])
    sign64 = jnp.concatenate([-jnp.ones((HALF_ROPE,), F32), jnp.ones((HALF_ROPE,), F32)])
    zeros64 = jnp.zeros((QK_ROPE,), F32)

    w_in = p["w_in"][l]
    c0 = 2 * CONV_CH
    c1 = c0 + Q_LORA
    c2 = c1 + KV_LORA
    c3 = c2 + QK_ROPE
    c4 = c3 + MEM_HEADS * MEM_HEAD_DIM
    w_in_t = w_in.T
    mid_t = bf(lax.optimization_barrier(w_in_t[c0:c4]))
    kpe_t = mid_t[c2 - c0:c3 - c0]
    kpe_swap_t = jnp.concatenate([kpe_t[HALF_ROPE:], kpe_t[:HALF_ROPE]], axis=0)
    w_small = jnp.concatenate(
        [mid_t[:c2 - c0], mid_t[c3 - c0:], kpe_t, kpe_t, kpe_swap_t, jnp.zeros_like(kpe_t)], axis=0).T

    w_uq = p["w_uq"][l].reshape(Q_LORA, MLA_HEADS, QK_DIM).transpose(1, 0, 2)
    wq = bf(jnp.concatenate([w_uq, _swap_halves(w_uq[..., QK_NOPE:])], axis=-1))
    w_kv = bf(p["w_ukv"][l].reshape(KV_LORA, MLA_HEADS, QK_NOPE + V_HEAD).transpose(1, 0, 2))

    g_q = p["g_qnorm"][l]
    g_k = p["g_knorm"][l]
    gq_rope, gk_rope = g_q[QK_NOPE:], g_k[QK_NOPE:]
    k_gc = jnp.concatenate([gk_rope, jnp.ones((QK_ROPE,), F32)])
    k_gs = jnp.concatenate([sign64 * _swap_halves(gk_rope), zeros64])
    q_g = jnp.concatenate([gq_rope, sign64 * _swap_halves(gq_rope)])
    freq128 = jnp.concatenate([freq64, freq64])

    x1, h = _ffn(x, row(p["g_ffn1"][l]), bf(p["w_ffn1_gu"][l]), bf(p["w_ffn1_down"][l]), row(p["g_mix"][l]),
                 emit_norm=True, tm=_tile(S, 512), tf=512)

    gates = _gates(h, w_in_t, row(p["b_gate"][l]), tm=_tile(S, 1024), tn=1024, row0=c4)
    a = _glu(h, w_in_t, tm=_tile(S, 1024), tn=512)
    cq, ckv, mq, kaux, qtab = _small_proj(
        h, w_small, pos, row(p["g_q_a"][l]), row(p["g_kv_a"][l]), row(p["g_mqnorm"][l]),
        row(freq128), row(k_gc), row(k_gs), row(q_g), tm=_tile(S, 512), col0=0)

    conv = _conv(a, p["conv_w"][l].astype(F32), row(p["conv_b"][l]), row(p["conv_ln_g"][l]),
                 row(p["conv_ln_b"][l]), tm=_tile(S, 256), rb=64)

    q, k, v = _qkv_prep(cq, ckv, kaux, qtab, wq, w_kv, row(g_q[:QK_NOPE]), row(g_k[:QK_NOPE]), tm=_tile(S, 512))
    bound = (QK_DIM ** 0.5 * LOG2E * 1.01) * jnp.max(jnp.abs(g_q)) * jnp.max(jnp.abs(g_k))
    sc = jnp.stack([bound, (bound <= MAX_STABLE_BOUND_LOG2).astype(F32)])
    o = _attention(sc, q, k, v, tq=_tile(S, 2048), tk=512, unroll=4)

    mk, mv = _mem_kv(mem, row(p["g_mem"][l]), bf(p["w_mem_kv"][l]), row(p["g_mknorm"][l]))
    om = _mem_attn(mq, mk, mv, tm=_tile(S, 512))

    merged = _merge(conv, o, om, gates, bf(p["w_conv_out"][l]), bf(p["w_mla_out"][l]), bf(p["w_mem_out"][l]),
                    tm=_tile(S, 512), tn=512)
    x2 = _out_proj(x1, merged, bf(p["w_out"][l]), tm=_tile(S, 512), tn=512)

    (x3,) = _ffn(x2, row(p["g_ffn2"][l]), bf(p["w_ffn2_gu"][l]), bf(p["w_ffn2_down"][l]), row(p["g_ffn2"][l]),
                 emit_norm=False, tm=_tile(S, 512), tf=512)
    return x3


def kernel(x, mem, positions, g_ffn1, w_ffn1_gu, w_ffn1_down, g_mix, w_in, b_gate, conv_w, conv_b, conv_ln_g, conv_ln_b, w_conv_out, g_q_a, w_uq, g_kv_a, w_ukv, g_qnorm, g_knorm, w_mla_out, g_mem, w_mem_kv, g_mqnorm, g_mknorm, w_mem_out, w_out, g_ffn2, w_ffn2_gu, w_ffn2_down):
    p = dict(g_ffn1=g_ffn1, w_ffn1_gu=w_ffn1_gu, w_ffn1_down=w_ffn1_down, g_mix=g_mix, w_in=w_in, b_gate=b_gate,
             conv_w=conv_w, conv_b=conv_b, conv_ln_g=conv_ln_g, conv_ln_b=conv_ln_b, w_conv_out=w_conv_out,
             g_q_a=g_q_a, w_uq=w_uq, g_kv_a=g_kv_a, w_ukv=w_ukv, g_qnorm=g_qnorm, g_knorm=g_knorm,
             w_mla_out=w_mla_out, g_mem=g_mem, w_mem_kv=w_mem_kv, g_mqnorm=g_mqnorm, g_mknorm=g_mknorm,
             w_mem_out=w_mem_out, w_out=w_out, g_ffn2=g_ffn2, w_ffn2_gu=w_ffn2_gu, w_ffn2_down=w_ffn2_down)
    B, S, D = x.shape
    depth = g_ffn1.shape[0]
    outs = []
    for b in range(B):
        xb = x.reshape(S, D) if B == 1 else x[b]
        pos = positions[b].reshape(S, 1)
        for l in range(depth):
            xb = _layer(xb, mem[b], pos, p, l)
        outs.append(xb.reshape(1, S, D))
    return outs[0] if B == 1 else jnp.concatenate(outs, axis=0)
```

```python
import functools

import jax
import jax.numpy as jnp
from jax import lax
from jax.experimental import pallas as pl
from jax.experimental.pallas import tpu as pltpu

F32 = jnp.float32
BF16 = jnp.bfloat16

D_MODEL = 2048
MEM_LEN = 256
D_FF = 5632
CONV_CH = 1536
CONV_WIDTH = 31
MLA_HEADS = 12
Q_LORA = 512
KV_LORA = 512
QK_NOPE = 128
QK_ROPE = 64
QK_DIM = QK_NOPE + QK_ROPE
V_HEAD = 128
MEM_HEADS = 4
MEM_HEAD_DIM = 256
N_BRANCH = 3
ROPE_THETA = 10000.0
EPS = 1e-6
HALF_ROPE = QK_ROPE // 2
LOG2E = 1.4426950408889634
MAX_STABLE_BOUND_LOG2 = 30.0 * LOG2E

VMEM_LIMIT_BYTES = 56 * 1024 * 1024
SUBLANES = 8
LANES = 128
CONV_HALO = 32


def _params(*sem):
    return pltpu.CompilerParams(dimension_semantics=sem, vmem_limit_bytes=VMEM_LIMIT_BYTES)


def _rms(x):
    return x * lax.rsqrt(jnp.mean(x * x, axis=-1, keepdims=True) + EPS)


def _dot(a, b):
    return jnp.dot(a, b, preferred_element_type=F32)


def _dot_nt(a, b):
    return lax.dot_general(a, b, (((1,), (1,)), ((), ())), preferred_element_type=F32)


def _ffn_body(x_ref, g_ref, wg_ref, wu_ref, wd_ref, gn_ref, o_ref, *rest, emit_norm):
    if emit_norm:
        h_ref, n_scr = rest
    else:
        (n_scr,) = rest
    j = pl.program_id(1)

    @pl.when(j == 0)
    def _():
        x = x_ref[...]
        n_scr[...] = (_rms(x) * g_ref[...]).astype(BF16)
        o_ref[...] = x

    n = n_scr[...]
    gg = _dot(n, wg_ref[...])
    uu = _dot(n, wu_ref[...])
    act = (gg * jax.nn.sigmoid(gg) * uu * 0.5).astype(BF16)
    o_ref[...] += _dot(act, wd_ref[...])

    if emit_norm:

        @pl.when(j == pl.num_programs(1) - 1)
        def _():
            h_ref[...] = (_rms(o_ref[...]) * gn_ref[...]).astype(BF16)


def _ffn(x, g, w_gu, w_down, g_next, *, emit_norm, tm, tf):
    S, D = x.shape
    nf = D_FF // tf
    out_shape = [jax.ShapeDtypeStruct((S, D), F32)]
    out_specs = [pl.BlockSpec((tm, D), lambda i, j: (i, 0))]
    if emit_norm:
        out_shape.append(jax.ShapeDtypeStruct((S, D), BF16))
        out_specs.append(pl.BlockSpec((tm, D), lambda i, j: (i, 0)))
    return pl.pallas_call(
        functools.partial(_ffn_body, emit_norm=emit_norm),
        grid=(S // tm, nf),
        in_specs=[
            pl.BlockSpec((tm, D), lambda i, j: (i, 0)),
            pl.BlockSpec((1, D), lambda i, j: (0, 0)),
            pl.BlockSpec((D, tf), lambda i, j: (0, j)),
            pl.BlockSpec((D, tf), lambda i, j: (0, j + nf)),
            pl.BlockSpec((tf, D), lambda i, j: (j, 0)),
            pl.BlockSpec((1, D), lambda i, j: (0, 0)),
        ],
        out_specs=out_specs,
        out_shape=out_shape,
        scratch_shapes=[pltpu.VMEM((tm, D), BF16)],
        compiler_params=_params("parallel", "arbitrary"),
        name="ffn_norm" if emit_norm else "ffn",
    )(x, g, w_gu, w_gu, w_down, g_next)


def _gates_body(h_ref, w_ref, wnext_ref, b_ref, o_ref, w_scr, *, shift):
    @pl.when(pl.program_id(1) == 0)
    def _():
        keep = w_scr.shape[0] - shift
        w_scr[0:keep, :] = w_ref[shift:, :].astype(BF16)
        if shift:
            w_scr[keep:, :] = wnext_ref[...].astype(BF16)

    o_ref[...] = jax.nn.sigmoid(_dot_nt(h_ref[...], w_scr[...]) + b_ref[...]).astype(BF16)


def _gates(h, wt, b, *, tm, tn, row0):
    S, D = h.shape
    N = b.shape[1]
    shift = row0 % tn
    j0 = row0 // tn
    assert shift % 16 == 0 and tn % max(shift, 1) == 0 and (row0 + N) % max(shift, 1) == 0
    halo = shift if shift else tn
    per = tn // halo
    return pl.pallas_call(
        functools.partial(_gates_body, shift=shift),
        grid=(N // tn, S // tm),
        in_specs=[
            pl.BlockSpec((tm, D), lambda j, i: (i, 0)),
            pl.BlockSpec((tn, D), lambda j, i: (j + j0, 0)),
            pl.BlockSpec((halo, D), lambda j, i: ((j + j0 + 1) * per, 0)),
            pl.BlockSpec((1, tn), lambda j, i: (0, j)),
        ],
        out_specs=pl.BlockSpec((tm, tn), lambda j, i: (i, j)),
        out_shape=jax.ShapeDtypeStruct((S, N), BF16),
        scratch_shapes=[pltpu.VMEM((tn, D), BF16)],
        compiler_params=_params("parallel", "arbitrary"),
        name="gates",
    )(h, wt, wt, b)


def _glu_body(h_ref, wa_ref, ws_ref, o_ref, wa_scr, ws_scr):
    @pl.when(pl.program_id(1) == 0)
    def _():
        wa_scr[...] = wa_ref[...].astype(BF16)
        ws_scr[...] = ws_ref[...].astype(BF16)

    h = h_ref[...]
    o_ref[...] = (_dot_nt(h, wa_scr[...]) * jax.nn.sigmoid(_dot_nt(h, ws_scr[...]))).astype(BF16)


def _glu(h, wt, *, tm, tn):
    S, D = h.shape
    nn = CONV_CH // tn
    return pl.pallas_call(
        _glu_body,
        grid=(nn, S // tm),
        in_specs=[
            pl.BlockSpec((tm, D), lambda j, i: (i, 0)),
            pl.BlockSpec((tn, D), lambda j, i: (j, 0)),
            pl.BlockSpec((tn, D), lambda j, i: (j + nn, 0)),
        ],
        out_specs=pl.BlockSpec((tm, tn), lambda j, i: (i, j)),
        out_shape=jax.ShapeDtypeStruct((S, CONV_CH), BF16),
        scratch_shapes=[pltpu.VMEM((tn, D), BF16), pltpu.VMEM((tn, D), BF16)],
        compiler_params=_params("parallel", "arbitrary"),
        name="glu",
    )(h, wt, wt)


_SP_CQ = 0
_SP_CKV = _SP_CQ + Q_LORA
_SP_MQ = _SP_CKV + KV_LORA
_SP_P1 = _SP_MQ + MEM_HEADS * MEM_HEAD_DIM
_SP_P2 = _SP_P1 + 2 * QK_ROPE
_SP_COLS = _SP_P2 + 2 * QK_ROPE


def _small_proj_body(h_ref, w_ref, pos_ref, gqa_ref, gkva_ref, gmq_ref, freq_ref, kgc_ref, kgs_ref, qg_ref,
                     cq_ref, ckv_ref, mq_ref, kaux_ref, qtab_ref):
    r = _dot(h_ref[...], w_ref[...])
    cq_ref[...] = (_rms(r[:, _SP_CQ:_SP_CKV]) * gqa_ref[...]).astype(BF16)
    ckv_ref[...] = (_rms(r[:, _SP_CKV:_SP_MQ]) * gkva_ref[...]).astype(BF16)
    for hd in range(MEM_HEADS):
        lo = hd * MEM_HEAD_DIM
        m = r[:, _SP_MQ + lo:_SP_MQ + lo + MEM_HEAD_DIM]
        mq_ref[:, lo:lo + MEM_HEAD_DIM] = (_rms(m) * gmq_ref[...] * (MEM_HEAD_DIM ** -0.5)).astype(BF16)
    ang = pos_ref[...].astype(F32) * freq_ref[...]
    cos, sin = jnp.cos(ang), jnp.sin(ang)
    low = lax.broadcasted_iota(jnp.int32, ang.shape, 1) < QK_ROPE
    kaux_ref[...] = (r[:, _SP_P1:_SP_P2] * (kgc_ref[...] * jnp.where(low, cos, 1.0))
                     + r[:, _SP_P2:_SP_COLS] * (kgs_ref[...] * sin))
    qtab_ref[...] = qg_ref[...] * jnp.where(low, cos, sin)


def _small_proj(h, w, pos, gqa, gkva, gmq, freq, kgc, kgs, qg, *, tm, col0):
    S, D = h.shape
    row = lambda n: pl.BlockSpec((1, n), lambda i: (0, 0))
    return pl.pallas_call(
        _small_proj_body,
        grid=(S // tm,),
        in_specs=[
            pl.BlockSpec((tm, D), lambda i: (i, 0)),
            pl.BlockSpec((D, _SP_COLS), lambda i: (0, col0 // _SP_COLS)),
            pl.BlockSpec((tm, 1), lambda i: (i, 0)),
            row(Q_LORA), row(KV_LORA), row(MEM_HEAD_DIM),
            row(2 * QK_ROPE), row(2 * QK_ROPE), row(2 * QK_ROPE), row(2 * QK_ROPE),
        ],
        out_specs=[
            pl.BlockSpec((tm, Q_LORA), lambda i: (i, 0)),
            pl.BlockSpec((tm, KV_LORA), lambda i: (i, 0)),
            pl.BlockSpec((tm, MEM_HEADS * MEM_HEAD_DIM), lambda i: (i, 0)),
            pl.BlockSpec((tm, 2 * QK_ROPE), lambda i: (i, 0)),
            pl.BlockSpec((tm, 2 * QK_ROPE), lambda i: (i, 0)),
        ],
        out_shape=[
            jax.ShapeDtypeStruct((S, Q_LORA), BF16),
            jax.ShapeDtypeStruct((S, KV_LORA), BF16),
            jax.ShapeDtypeStruct((S, MEM_HEADS * MEM_HEAD_DIM), BF16),
            jax.ShapeDtypeStruct((S, 2 * QK_ROPE), F32),
            jax.ShapeDtypeStruct((S, 2 * QK_ROPE), F32),
        ],
        compiler_params=_params("parallel"),
        name="small_proj",
    )(h, w, pos, gqa, gkva, gmq, freq, kgc, kgs, qg)


def _qkv_body(cq_ref, ckv_ref, kaux_ref, qtab_ref, wq_ref, wkv_ref, ones_ref, gqn_ref, gkn_ref, q_ref, k_ref, v_ref):
    cq = cq_ref[...]
    ckv = ckv_ref[...]
    kaux = kaux_ref[...]
    qtab = qtab_ref[...]
    low = lax.broadcasted_iota(jnp.int32, kaux.shape, 1) < QK_ROPE
    kss = jnp.sum(jnp.where(low, 0.0, kaux * kaux), axis=-1, keepdims=True)
    for hd in range(MLA_HEADS):
        qf = _dot(cq, wq_ref[hd])
        kv = _dot(ckv, wkv_ref[hd])
        nope, rot = qf[:, 0:QK_NOPE], qf[:, QK_NOPE:]
        kn = kv[:, 0:QK_NOPE]
        sq = jnp.concatenate([nope * nope + jnp.where(low, rot * rot, 0.0), kn * kn], axis=1)
        hi = sq.astype(BF16)
        lo = (sq - hi.astype(F32)).astype(BF16)
        tot = _dot(hi, ones_ref[...]) + _dot(lo, ones_ref[...])
        rq = lax.rsqrt(tot[:, 0:QK_NOPE] * (1.0 / QK_DIM) + EPS) * (QK_DIM ** -0.5 * LOG2E)
        rk = lax.rsqrt((tot[:, QK_NOPE:] + kss) * (1.0 / QK_DIM) + EPS)
        q_ref[hd, :, 0:QK_NOPE] = (nope * rq * gqn_ref[...]).astype(BF16)
        mixed = rot * qtab
        mixed = mixed + pltpu.roll(mixed, QK_ROPE, axis=1)
        q_ref[hd, :, QK_NOPE:QK_DIM] = (mixed * rq)[:, 0:QK_ROPE].astype(BF16)
        k_ref[hd, :, 0:QK_NOPE] = (kn * rk * gkn_ref[...]).astype(BF16)
        k_ref[hd, :, QK_NOPE:QK_DIM] = (kaux * rk)[:, 0:QK_ROPE].astype(BF16)
        v_ref[hd] = kv[:, QK_NOPE:QK_NOPE + V_HEAD].astype(BF16)


def _qkv_prep(cq, ckv, kaux, qtab, wq, wkv, ones2, gqn, gkn, *, tm):
    S = cq.shape[0]
    H = MLA_HEADS
    row = lambda n: pl.BlockSpec((1, n), lambda i: (0, 0))
    return pl.pallas_call(
        _qkv_body,
        grid=(S // tm,),
        in_specs=[
            pl.BlockSpec((tm, Q_LORA), lambda i: (i, 0)),
            pl.BlockSpec((tm, KV_LORA), lambda i: (i, 0)),
            pl.BlockSpec((tm, 2 * QK_ROPE), lambda i: (i, 0)),
            pl.BlockSpec((tm, 2 * QK_ROPE), lambda i: (i, 0)),
            pl.BlockSpec((H, Q_LORA, QK_DIM + QK_ROPE), lambda i: (0, 0, 0)),
            pl.BlockSpec((H, KV_LORA, QK_NOPE + V_HEAD), lambda i: (0, 0, 0)),
            pl.BlockSpec((2 * QK_NOPE, 2 * QK_NOPE), lambda i: (0, 0)),
            row(QK_NOPE), row(QK_NOPE),
        ],
        out_specs=[
            pl.BlockSpec((H, tm, QK_DIM), lambda i: (0, i, 0)),
            pl.BlockSpec((H, tm, QK_DIM), lambda i: (0, i, 0)),
            pl.BlockSpec((H, tm, V_HEAD), lambda i: (0, i, 0)),
        ],
        out_shape=[
            jax.ShapeDtypeStruct((H, S, QK_DIM), BF16),
            jax.ShapeDtypeStruct((H, S, QK_DIM), BF16),
            jax.ShapeDtypeStruct((H, S, V_HEAD), BF16),
        ],
        compiler_params=_params("parallel"),
        name="qkv_prep",
    )(cq, ckv, kaux, qtab, wq, wkv, ones2, gqn, gkn)


def _attn_body(sc_ref, q_ref, k_ref, v_ref, o_ref, l_scr, acc_scr, m_scr, *, tq, tk, unroll):
    qi = pl.program_id(1)
    q = q_ref[0]
    n_diag = tq // tk
    n_full = qi * n_diag

    def kv_block(kb):
        start = pl.multiple_of(kb * tk, tk)
        return k_ref[0, pl.ds(start, tk), :], v_ref[0, pl.ds(start, tk), :]

    def causal(x, kb, fill, row0=0):
        row = lax.broadcasted_iota(jnp.int32, x.shape, 0) + (qi * tq + row0)
        col = lax.broadcasted_iota(jnp.int32, x.shape, 1) + kb * tk
        return jnp.where(col <= row, x, fill)

    @pl.when(sc_ref[1] > 0.0)
    def _bounded():
        bound = sc_ref[0]

        def block(kb, masked, row0=0):
            k, v = kv_block(kb)
            p = jnp.exp2(_dot_nt(q[row0:], k) - bound)
            if masked:
                p = causal(p, kb, 0.0, row0)
            lanes = p[:, 0:128]
            for c in range(128, tk, 128):
                lanes = lanes + p[:, c:c + 128]
            return lanes, _dot(p.astype(BF16), v)

        def add(kb, n, masked=False):
            lanes, pv = block(kb, masked)
            for u in range(1, n):
                lanes_u, pv_u = block(kb + u, masked)
                lanes, pv = lanes + lanes_u, pv + pv_u
            return lanes, pv

        l_scr[...] = jnp.zeros(l_scr.shape, F32)
        acc_scr[...] = jnp.zeros(acc_scr.shape, F32)

        def many(sb, carry):
            lanes, pv = add(sb * unroll, unroll)
            l_scr[...] += lanes
            acc_scr[...] += pv
            return carry

        def one(kb, carry):
            lanes, pv = add(kb, 1)
            l_scr[...] += lanes
            acc_scr[...] += pv
            return carry

        n_many = n_full // unroll
        lax.fori_loop(0, n_many, many, 0)
        lax.fori_loop(n_many * unroll, n_full, one, 0)
        for d in range(n_diag):
            lanes, pv = block(n_full + d, True, row0=d * tk)
            l_scr[d * tk:, :] += lanes
            acc_scr[d * tk:, :] += pv
        l = jnp.sum(l_scr[...], axis=-1, keepdims=True)
        o_ref[...] = (acc_scr[...] / l).astype(BF16)

    @pl.when(sc_ref[1] <= 0.0)
    def _online():
        m_scr[...] = jnp.full(m_scr.shape, -jnp.inf, F32)
        l_scr[...] = jnp.zeros(l_scr.shape, F32)
        acc_scr[...] = jnp.zeros(acc_scr.shape, F32)

        def step(kb, masked):
            k, v = kv_block(kb)
            s = _dot_nt(q, k)
            if masked:
                s = causal(s, kb, -jnp.inf)
            m_old = m_scr[...]
            m_new = jnp.maximum(m_old, jnp.max(s, axis=-1, keepdims=True))
            p = jnp.exp2(s - m_new)
            alpha = jnp.exp2(m_old - m_new)
            l_scr[:, 0:1] = alpha * l_scr[:, 0:1] + jnp.sum(p, axis=-1, keepdims=True)
            acc_scr[...] = alpha * acc_scr[...] + _dot(p.astype(BF16), v)
            m_scr[...] = m_new

        def body(kb, carry):
            step(kb, False)
            return carry

        lax.fori_loop(0, n_full, body, 0)
        for d in range(n_diag):
            step(n_full + d, True)
        o_ref[...] = (acc_scr[...] / l_scr[:, 0:1]).astype(BF16)


def _attention(sc, q, k, v, *, tq, tk, unroll):
    H, S, _ = q.shape
    return pl.pallas_call(
        functools.partial(_attn_body, tq=tq, tk=tk, unroll=unroll),
        grid_spec=pltpu.PrefetchScalarGridSpec(
            num_scalar_prefetch=1,
            grid=(H, S // tq),
            in_specs=[
                pl.BlockSpec((1, tq, QK_DIM), lambda h, i, sc: (h, i, 0)),
                pl.BlockSpec((1, S, QK_DIM), lambda h, i, sc: (h, 0, 0)),
                pl.BlockSpec((1, S, V_HEAD), lambda h, i, sc: (h, 0, 0)),
            ],
            out_specs=pl.BlockSpec((tq, V_HEAD), lambda h, i, sc: (i, h)),
            scratch_shapes=[pltpu.VMEM((tq, 128), F32), pltpu.VMEM((tq, V_HEAD), F32), pltpu.VMEM((tq, 1), F32)],
        ),
        out_shape=jax.ShapeDtypeStruct((S, H * V_HEAD), BF16),
        compiler_params=_params("parallel", "arbitrary"),
        name="attention",
    )(sc, q, k, v)


def _conv_body(cur_ref, halo_ref, w_ref, b_ref, lg_ref, lb_ref, o_ref, ext_scr, sh_scr, conv_scr, *, tm, rb):
    halo = halo_ref[...].astype(F32)
    ext_scr[0:CONV_HALO, :] = jnp.where(pl.program_id(0) == 0, 0.0, halo)
    ext_scr[CONV_HALO:CONV_HALO + tm, :] = cur_ref[...].astype(F32)
    n_sh = sh_scr.shape[1]
    for r in range(1, SUBLANES):
        sh_scr[r - 1] = ext_scr[r:r + n_sh, :]
    base = CONV_HALO - (CONV_WIDTH - 1)
    for c0 in range(0, CONV_CH, LANES):
        for r0 in range(0, tm, rb):
            acc = jnp.broadcast_to(b_ref[:, c0:c0 + LANES], (rb, LANES))
            for j in range(CONV_WIDTH):
                r = (base + j) % SUBLANES
                lo = r0 + base + j - r
                if r == 0:
                    rows = ext_scr[lo:lo + rb, c0:c0 + LANES]
                else:
                    rows = sh_scr[r - 1, lo:lo + rb, c0:c0 + LANES]
                acc = acc + rows * w_ref[j:j + 1, c0:c0 + LANES]
            conv_scr[r0:r0 + rb, c0:c0 + LANES] = acc
    a = conv_scr[...]
    xc = a - jnp.mean(a, axis=-1, keepdims=True)
    y = xc * lax.rsqrt(jnp.mean(xc * xc, axis=-1, keepdims=True) + EPS) * lg_ref[...] + lb_ref[...]
    o_ref[...] = (y * jax.nn.sigmoid(y)).astype(BF16)


def _conv(a, w, b, lg, lb, *, tm, rb):
    S, C = a.shape
    hb = tm // CONV_HALO
    row = lambda: pl.BlockSpec((1, C), lambda i: (0, 0))
    return pl.pallas_call(
        functools.partial(_conv_body, tm=tm, rb=rb),
        grid=(S // tm,),
        in_specs=[
            pl.BlockSpec((tm, C), lambda i: (i, 0)),
            pl.BlockSpec((CONV_HALO, C), lambda i: (jnp.maximum(i * hb - 1, 0), 0)),
            pl.BlockSpec((CONV_WIDTH, C), lambda i: (0, 0)),
            row(), row(), row(),
        ],
        out_specs=pl.BlockSpec((tm, C), lambda i: (i, 0)),
        out_shape=jax.ShapeDtypeStruct((S, C), BF16),
        scratch_shapes=[
            pltpu.VMEM((CONV_HALO + tm, C), F32),
            pltpu.VMEM((SUBLANES - 1, CONV_HALO + tm - SUBLANES, C), F32),
            pltpu.VMEM((tm, C), F32),
        ],
        compiler_params=_params("parallel"),
        name="conv",
    )(a, a, w, b, lg, lb)


def _mem_kv_body(mem_ref, g_ref, w_ref, gk_ref, mk_ref, mv_ref):
    n = (_rms(mem_ref[...]) * g_ref[...]).astype(BF16)
    kv = _dot(n, w_ref[...])
    width = MEM_HEADS * MEM_HEAD_DIM
    for hd in range(MEM_HEADS):
        lo = hd * MEM_HEAD_DIM
        mk_ref[:, lo:lo + MEM_HEAD_DIM] = (_rms(kv[:, lo:lo + MEM_HEAD_DIM]) * gk_ref[...]).astype(BF16)
    mv_ref[...] = kv[:, width:2 * width].astype(BF16)


def _mem_kv(mem, g, w, gk):
    M, D = mem.shape
    width = MEM_HEADS * MEM_HEAD_DIM
    full = lambda shape: pl.BlockSpec(shape, lambda i: (0,) * len(shape))
    return pl.pallas_call(
        _mem_kv_body,
        grid=(1,),
        in_specs=[full((M, D)), full((1, D)), full((D, 2 * width)), full((1, MEM_HEAD_DIM))],
        out_specs=[full((M, width)), full((M, width))],
        out_shape=[jax.ShapeDtypeStruct((M, width), BF16), jax.ShapeDtypeStruct((M, width), BF16)],
        compiler_params=_params("arbitrary"),
        name="mem_kv",
    )(mem, g, w, gk)


def _mem_attn_body(mq_ref, mk_ref, mv_ref, o_ref):
    for hd in range(MEM_HEADS):
        lo = hd * MEM_HEAD_DIM
        s = _dot_nt(mq_ref[:, lo:lo + MEM_HEAD_DIM], mk_ref[:, lo:lo + MEM_HEAD_DIM])
        p = jnp.exp(s - jnp.max(s, axis=-1, keepdims=True))
        p = p / jnp.sum(p, axis=-1, keepdims=True)
        o_ref[:, lo:lo + MEM_HEAD_DIM] = _dot(p.astype(BF16), mv_ref[:, lo:lo + MEM_HEAD_DIM]).astype(BF16)


def _mem_attn(mq, mk, mv, *, tm):
    S, W = mq.shape
    M = mk.shape[0]
    return pl.pallas_call(
        _mem_attn_body,
        grid=(S // tm,),
        in_specs=[
            pl.BlockSpec((tm, W), lambda i: (i, 0)),
            pl.BlockSpec((M, W), lambda i: (0, 0)),
            pl.BlockSpec((M, W), lambda i: (0, 0)),
        ],
        out_specs=pl.BlockSpec((tm, W), lambda i: (i, 0)),
        out_shape=jax.ShapeDtypeStruct((S, W), BF16),
        compiler_params=_params("parallel"),
        name="mem_attn",
    )(mq, mk, mv)


def _resident(shape):
    return pl.BlockSpec(shape, lambda i: (0,) * len(shape), pipeline_mode=pl.Buffered(1))


def _merge_body(c_ref, o_ref, om_ref, g0_ref, g1_ref, g2_ref, wc_ref, wo_ref, wm_ref, out_ref, *, tn):
    c, o, om = c_ref[...], o_ref[...], om_ref[...]
    for n0 in range(0, out_ref.shape[1], tn):
        cols = slice(n0, n0 + tn)
        out_ref[:, cols] = (g0_ref[:, cols].astype(F32) * _dot(c, wc_ref[:, cols])
                            + g1_ref[:, cols].astype(F32) * _dot(o, wo_ref[:, cols])
                            + g2_ref[:, cols].astype(F32) * _dot(om, wm_ref[:, cols])).astype(BF16)


def _merge(c, o, om, gates, wc, wo, wm, *, tm, tn):
    S = c.shape[0]
    D = wc.shape[1]
    rows = lambda a: pl.BlockSpec((tm, a.shape[1]), lambda i: (i, 0))
    return pl.pallas_call(
        functools.partial(_merge_body, tn=tn),
        grid=(S // tm,),
        in_specs=[
            rows(c), rows(o), rows(om),
            pl.BlockSpec((tm, D), lambda i: (i, 0)),
            pl.BlockSpec((tm, D), lambda i: (i, 1)),
            pl.BlockSpec((tm, D), lambda i: (i, 2)),
            _resident(wc.shape), _resident(wo.shape), _resident(wm.shape),
        ],
        out_specs=pl.BlockSpec((tm, D), lambda i: (i, 0)),
        out_shape=jax.ShapeDtypeStruct((S, D), BF16),
        compiler_params=_params("parallel"),
        name="merge",
    )(c, o, om, gates, gates, gates, wc, wo, wm)


def _out_proj_body(x_ref, m_ref, w_ref, out_ref, *, tn):
    m = m_ref[...]
    for n0 in range(0, out_ref.shape[1], tn):
        cols = slice(n0, n0 + tn)
        out_ref[:, cols] = x_ref[:, cols] + _dot(m, w_ref[:, cols])


def _out_proj(x, m, w, *, tm, tn):
    S, D = x.shape
    return pl.pallas_call(
        functools.partial(_out_proj_body, tn=tn),
        grid=(S // tm,),
        in_specs=[
            pl.BlockSpec((tm, D), lambda i: (i, 0)),
            pl.BlockSpec((tm, D), lambda i: (i, 0)),
            _resident(w.shape),
        ],
        out_specs=pl.BlockSpec((tm, D), lambda i: (i, 0)),
        out_shape=jax.ShapeDtypeStruct((S, D), F32),
        compiler_params=_params("parallel"),
        name="out_proj",
    )(x, m, w)


def _swap_halves(v):
    return jnp.concatenate([v[..., HALF_ROPE:], v[..., :HALF_ROPE]], axis=-1)


def _tile(S, want):
    return min(S, want)


def _layer(x, mem, pos, p, l):
    S = x.shape[0]
    row = lambda v: v.reshape(1, -1).astype(F32)
    bf = lambda w: w.astype(BF16)

    inv_freq = ROPE_THETA ** (-jnp.arange(0, QK_ROPE, 2, dtype=F32) / QK_ROPE)
    freq64 = jnp.concatenate([inv_freq, inv_freq])
    sign64 = jnp.concatenate([-jnp.ones((HALF_ROPE,), F32), jnp.ones((HALF_ROPE,), F32)])
    zeros64 = jnp.zeros((QK_ROPE,), F32)

    w_in = p["w_in"][l]
    c0 = 2 * CONV_CH
    c1 = c0 + Q_LORA
    c2 = c1 + KV_LORA
    c3 = c2 + QK_ROPE
    c4 = c3 + MEM_HEADS * MEM_HEAD_DIM
    w_in_t = w_in.T
    mid_t = bf(lax.optimization_barrier(w_in_t[c0:c4]))
    kpe_t = mid_t[c2 - c0:c3 - c0]
    kpe_swap_t = jnp.concatenate([kpe_t[HALF_ROPE:], kpe_t[:HALF_ROPE]], axis=0)
    w_small = jnp.concatenate(
        [mid_t[:c2 - c0], mid_t[c3 - c0:], kpe_t, kpe_t, kpe_swap_t, jnp.zeros_like(kpe_t)], axis=0).T

    w_uq = p["w_uq"][l].reshape(Q_LORA, MLA_HEADS, QK_DIM).transpose(1, 0, 2)
    wq = bf(jnp.concatenate([w_uq, _swap_halves(w_uq[..., QK_NOPE:])], axis=-1))
    w_kv = bf(p["w_ukv"][l].reshape(KV_LORA, MLA_HEADS, QK_NOPE + V_HEAD).transpose(1, 0, 2))

    g_q = p["g_qnorm"][l]
    g_k = p["g_knorm"][l]
    gq_rope, gk_rope = g_q[QK_NOPE:], g_k[QK_NOPE:]
    k_gc = jnp.concatenate([gk_rope, jnp.ones((QK_ROPE,), F32)])
    k_gs = jnp.concatenate([sign64 * _swap_halves(gk_rope), zeros64])
    q_g = jnp.concatenate([gq_rope, sign64 * _swap_halves(gq_rope)])
    freq128 = jnp.concatenate([freq64, freq64])

    x1, h = _ffn(x, row(p["g_ffn1"][l]), bf(p["w_ffn1_gu"][l]), bf(p["w_ffn1_down"][l]), row(p["g_mix"][l]),
                 emit_norm=True, tm=_tile(S, 512), tf=512)

    gates = _gates(h, w_in_t, row(p["b_gate"][l]), tm=_tile(S, 1024), tn=1024, row0=c4)
    a = _glu(h, w_in_t, tm=_tile(S, 1024), tn=512)
    cq, ckv, mq, kaux, qtab = _small_proj(
        h, w_small, pos, row(p["g_q_a"][l]), row(p["g_kv_a"][l]), row(p["g_mqnorm"][l]),
        row(freq128), row(k_gc), row(k_gs), row(q_g), tm=_tile(S, 512), col0=0)

    conv = _conv(a, p["conv_w"][l].astype(F32), row(p["conv_b"][l]), row(p["conv_ln_g"][l]),
                 row(p["conv_ln_b"][l]), tm=_tile(S, 256), rb=64)

    half = jnp.arange(2 * QK_NOPE) // QK_NOPE
    ones2 = (half[:, None] == half[None, :]).astype(BF16)
    q, k, v = _qkv_prep(cq, ckv, kaux, qtab, wq, w_kv, ones2, row(g_q[:QK_NOPE]), row(g_k[:QK_NOPE]),
                        tm=_tile(S, 512))
    bound = (QK_DIM ** 0.5 * LOG2E * 1.01) * jnp.max(jnp.abs(g_q)) * jnp.max(jnp.abs(g_k))
    sc = jnp.stack([bound, (bound <= MAX_STABLE_BOUND_LOG2).astype(F32)])
    o = _attention(sc, q, k, v, tq=_tile(S, 2048), tk=512, unroll=4)

    mk, mv = _mem_kv(mem, row(p["g_mem"][l]), bf(p["w_mem_kv"][l]), row(p["g_mknorm"][l]))
    om = _mem_attn(mq, mk, mv, tm=_tile(S, 512))

    merged = _merge(conv, o, om, gates, bf(p["w_conv_out"][l]), bf(p["w_mla_out"][l]), bf(p["w_mem_out"][l]),
                    tm=_tile(S, 512), tn=512)
    x2 = _out_proj(x1, merged, bf(p["w_out"][l]), tm=_tile(S, 512), tn=512)

    (x3,) = _ffn(x2, row(p["g_ffn2"][l]), bf(p["w_ffn2_gu"][l]), bf(p["w_ffn2_down"][l]), row(p["g_ffn2"][l]),
                 emit_norm=False, tm=_tile(S, 512), tf=512)
    return x3


def kernel(x, mem, positions, g_ffn1, w_ffn1_gu, w_ffn1_down, g_mix, w_in, b_gate, conv_w, conv_b, conv_ln_g, conv_ln_b, w_conv_out, g_q_a, w_uq, g_kv_a, w_ukv, g_qnorm, g_knorm, w_mla_out, g_mem, w_mem_kv, g_mqnorm, g_mknorm, w_mem_out, w_out, g_ffn2, w_ffn2_gu, w_ffn2_down):
    p = dict(g_ffn1=g_ffn1, w_ffn1_gu=w_ffn1_gu, w_ffn1_down=w_ffn1_down, g_mix=g_mix, w_in=w_in, b_gate=b_gate,
             conv_w=conv_w, conv_b=conv_b, conv_ln_g=conv_ln_g, conv_ln_b=conv_ln_b, w_conv_out=w_conv_out,
             g_q_a=g_q_a, w_uq=w_uq, g_kv_a=g_kv_a, w_ukv=w_ukv, g_qnorm=g_qnorm, g_knorm=g_knorm,
             w_mla_out=w_mla_out, g_mem=g_mem, w_mem_kv=w_mem_kv, g_mqnorm=g_mqnorm, g_mknorm=g_mknorm,
             w_mem_out=w_mem_out, w_out=w_out, g_ffn2=g_ffn2, w_ffn2_gu=w_ffn2_gu, w_ffn2_down=w_ffn2_down)
    B, S, D = x.shape
    depth = g_ffn1.shape[0]
    outs = []
    for b in range(B):
        xb = x.reshape(S, D) if B == 1 else x[b]
        pos = positions[b].reshape(S, 1)
        for l in range(depth):
            xb = _layer(xb, mem[b], pos, p, l)
        outs.append(xb.reshape(1, S, D))
    return outs[0] if B == 1 else jnp.concatenate(outs, axis=0)
```
